```python
import math
import jax, jax.numpy as jnp
from jax import lax
import numpy as np

D_MODEL = 2048
BATCH = 2
SEQ = 4096
DEPTH = 4
DEC_BATCH = 8
DEC_SEQ = 4
PAST_LEN = 16384
PAGE_SIZE = 128

N_MIXERS = 2
N_FOX = (DEPTH + 1) // 2
N_LRU = DEPTH // 2
HEAD_DIM = 128
N_HEADS = D_MODEL // HEAD_DIM
Q_BLOCK = 128
ATTN_SCALE = 1.0 / math.sqrt(HEAD_DIM)
D_RNN = ((4 * D_MODEL // 3 + 255) // 256) * 256
N_BLOCKS = 16
BLOCK_W = D_RNN // N_BLOCKS
CONV_W = 4
C_RG = 8.0
D_FF = ((8 * D_MODEL // 3 + 255) // 256) * 256
RMS_EPS = 1e-6
NEG_INF = -1e30

kernel_name = "fox_rglru_hybrid_step"


def rms_norm(x, g):
    xf = x.astype(jnp.float32)
    y = xf * lax.rsqrt(jnp.mean(xf * xf, axis=-1, keepdims=True) + RMS_EPS)
    return (y * g.astype(jnp.float32)).astype(x.dtype)


def swiglu(x, w_gate, w_up, w_down):
    return (jax.nn.silu(x @ w_gate) * (x @ w_up)) @ w_down


def fox_project(xn, w_qkv, w_f, b_f):
    b, t, _ = xn.shape
    q, k, v = jnp.split(xn @ w_qkv, 3, axis=-1)
    q = q.reshape(b, t, N_HEADS, HEAD_DIM)
    k = k.reshape(b, t, N_HEADS, HEAD_DIM)
    v = v.reshape(b, t, N_HEADS, HEAD_DIM)
    logf = jax.nn.log_sigmoid((xn @ w_f + b_f).astype(jnp.float32))
    return q, k, v, logf


def fox_prompt(xn, w_qkv, w_f, b_f, w_o):
    b, s, _ = xn.shape
    q, k, v, logf = fox_project(xn, w_qkv, w_f, b_f)
    c_t = jnp.cumsum(logf, axis=1).transpose(0, 2, 1)
    key_pos = jnp.arange(s)

    def block(bi):
        start = bi * Q_BLOCK
        qb = lax.dynamic_slice_in_dim(q, start, Q_BLOCK, axis=1)
        cq = lax.dynamic_slice_in_dim(c_t, start, Q_BLOCK, axis=2)
        logits = jnp.einsum('bqhd,bkhd->bhqk', qb, k).astype(jnp.float32) * ATTN_SCALE
        logits = logits + cq[..., :, None] - c_t[..., None, :]
        q_pos = start + jnp.arange(Q_BLOCK)
        mask = key_pos[None, :] <= q_pos[:, None]
        p = jax.nn.softmax(jnp.where(mask, logits, NEG_INF), axis=-1)
        return jnp.einsum('bhqk,bkhd->bqhd', p.astype(v.dtype), v)

    o = lax.map(block, jnp.arange(s // Q_BLOCK))
    o = o.transpose(1, 0, 2, 3, 4).reshape(b, s, D_MODEL)
    return o @ w_o, k, v, logf


def fox_sample(xn, k_pages, v_pages, logf_pages, page_table, w_qkv, w_f, b_f, w_o):
    b, t, _ = xn.shape
    q, k, v, logf = fox_project(xn, w_qkv, w_f, b_f)
    k_past = k_pages[page_table].reshape(b, -1, N_HEADS, HEAD_DIM)
    v_past = v_pages[page_table].reshape(b, -1, N_HEADS, HEAD_DIM)
    lf_past = logf_pages[page_table].reshape(b, -1, N_HEADS).astype(jnp.float32)
    n_past = k_past.shape[1]
    suffix = (lax.cumsum(lf_past, axis=1, reverse=True) - lf_past).transpose(0, 2, 1)
    c_new = jnp.cumsum(logf, axis=1).transpose(0, 2, 1)
    s_past = jnp.einsum('bthd,bshd->bhts', q, k_past).astype(jnp.float32) * ATTN_SCALE
    s_past = s_past + c_new[..., :, None] + suffix[..., None, :]
    s_new = jnp.einsum('bthd,bshd->bhts', q, k).astype(jnp.float32) * ATTN_SCALE
    s_new = s_new + c_new[..., :, None] - c_new[..., None, :]
    mask = jnp.tril(jnp.ones((t, t), dtype=bool))
    s_new = jnp.where(mask, s_new, NEG_INF)
    p = jax.nn.softmax(jnp.concatenate([s_past, s_new], axis=-1), axis=-1)
    o = (jnp.einsum('bhts,bshd->bthd', p[..., :n_past].astype(v.dtype), v_past)
         + jnp.einsum('bhts,bshd->bthd', p[..., n_past:].astype(v.dtype), v))
    return o.reshape(b, t, D_MODEL) @ w_o, k, v, logf


def rglru_mixer(xn, conv_buf, h0, w_in, conv_w, conv_b, w_a, b_a, w_i, b_i, lam, w_out):
    b, t, _ = xn.shape
    gate_in, rec = jnp.split(xn @ w_in, 2, axis=-1)
    full = jnp.concatenate([conv_buf.astype(rec.dtype), rec], axis=1)
    conv = conv_b + full[:, 0:t] * conv_w[0]
    for tap in range(1, CONV_W):
        conv = conv + full[:, tap:tap + t] * conv_w[tap]
    new_buf = full[:, t:]
    xb = conv.reshape(b, t, N_BLOCKS, BLOCK_W)
    r = jax.nn.sigmoid((jnp.einsum('btnk,nkj->btnj', xb, w_a).reshape(b, t, D_RNN) + b_a).astype(jnp.float32))
    ig = jax.nn.sigmoid((jnp.einsum('btnk,nkj->btnj', xb, w_i).reshape(b, t, D_RNN) + b_i).astype(jnp.float32))
    log_a = -C_RG * r * jax.nn.softplus(-lam.astype(jnp.float32))
    a = jnp.exp(log_a)
    xin = jnp.sqrt(-jnp.expm1(2.0 * log_a)) * ig * conv.astype(jnp.float32)

    def step(h, inp):
        a_t, x_t = inp
        h = a_t * h + x_t
        return h, h

    h_last, hs = lax.scan(step, h0.astype(jnp.float32), (a.swapaxes(0, 1), xin.swapaxes(0, 1)))
    hseq = hs.swapaxes(0, 1).astype(xn.dtype)
    y = (hseq * jax.nn.gelu(gate_in, approximate=True)) @ w_out
    return y, new_buf, h_last


def setup_inputs(seed: int = 0) -> dict:
    key = jax.random.key(seed)
    ks = jax.random.split(key, 32)
    f32 = jnp.float32
    n_pages = PAST_LEN // PAGE_SIZE
    n_used = DEC_BATCH * n_pages
    n_pool = n_used + n_used // 4

    def nrm(k, shape, scale):
        return jax.random.normal(k, shape, f32) * scale

    def gain(k, shape):
        return 1.0 + 0.1 * jax.random.normal(k, shape, f32)

    x_prompt = nrm(ks[0], (BATCH, SEQ, D_MODEL), 1.0)
    x_sample = nrm(ks[1], (DEC_BATCH, DEC_SEQ, D_MODEL), 1.0)
    cache_k = nrm(ks[2], (N_FOX, n_pool, PAGE_SIZE, N_HEADS, HEAD_DIM), 1.0)
    cache_v = nrm(ks[3], (N_FOX, n_pool, PAGE_SIZE, N_HEADS, HEAD_DIM), 1.0)
    cache_logf = jax.nn.log_sigmoid(3.0 + jax.random.normal(ks[4], (N_FOX, n_pool, PAGE_SIZE, N_HEADS), f32))
    state_conv = nrm(ks[5], (N_LRU, DEC_BATCH, CONV_W - 1, D_RNN), 1.0)
    state_h = nrm(ks[6], (N_LRU, DEC_BATCH, D_RNN), 0.5)
    page_table = jax.random.permutation(ks[7], n_pool)[:n_used].reshape(DEC_BATCH, n_pages).astype(jnp.int32)

    u = jax.random.uniform(ks[8], (N_LRU, D_RNN), f32, minval=0.9, maxval=0.999)
    a_base = u ** (1.0 / C_RG)
    lru_lam = jnp.log(a_base) - jnp.log1p(-a_base)

    return {
        "x_prompt": x_prompt,
        "x_sample": x_sample,
        "cache_k": cache_k,
        "cache_v": cache_v,
        "cache_logf": cache_logf,
        "state_conv": state_conv,
        "state_h": state_h,
        "page_table": page_table,
        "norm_mix_pre": gain(ks[9], (DEPTH, D_MODEL)),
        "norm_mix_post": gain(ks[10], (DEPTH, D_MODEL)),
        "norm_ffn_pre": gain(ks[11], (DEPTH, D_MODEL)),
        "norm_ffn_post": gain(ks[12], (DEPTH, D_MODEL)),
        "fox_w_qkv": nrm(ks[13], (N_FOX, D_MODEL, 3 * D_MODEL), D_MODEL ** -0.5),
        "fox_w_f": nrm(ks[14], (N_FOX, D_MODEL, N_HEADS), D_MODEL ** -0.5),
        "fox_b_f": jax.random.uniform(ks[15], (N_FOX, N_HEADS), f32, minval=1.0, maxval=4.0),
        "fox_w_o": nrm(ks[16], (N_FOX, D_MODEL, D_MODEL), D_MODEL ** -0.5),
        "lru_w_in": nrm(ks[17], (N_LRU, D_MODEL, 2 * D_RNN), D_MODEL ** -0.5),
        "lru_conv_w": nrm(ks[18], (N_LRU, CONV_W, D_RNN), CONV_W ** -0.5),
        "lru_conv_b": nrm(ks[19], (N_LRU, D_RNN), 0.01),
        "lru_w_a": nrm(ks[20], (N_LRU, N_BLOCKS, BLOCK_W, BLOCK_W), BLOCK_W ** -0.5),
        "lru_b_a": nrm(ks[21], (N_LRU, D_RNN), 0.01),
        "lru_w_i": nrm(ks[22], (N_LRU, N_BLOCKS, BLOCK_W, BLOCK_W), BLOCK_W ** -0.5),
        "lru_b_i": nrm(ks[23], (N_LRU, D_RNN), 0.01),
        "lru_lam": lru_lam,
        "lru_w_out": nrm(ks[24], (N_LRU, D_RNN, D_MODEL), D_RNN ** -0.5),
        "ffn_w_gate": nrm(ks[25], (DEPTH, D_MODEL, D_FF), D_MODEL ** -0.5),
        "ffn_w_up": nrm(ks[26], (DEPTH, D_MODEL, D_FF), D_MODEL ** -0.5),
        "ffn_w_down": nrm(ks[27], (DEPTH, D_FF, D_MODEL), D_FF ** -0.5),
    }


def reference(x_prompt, x_sample, cache_k, cache_v, cache_logf, state_conv, state_h, page_table,
              norm_mix_pre, norm_mix_post, norm_ffn_pre, norm_ffn_post,
              fox_w_qkv, fox_w_f, fox_b_f, fox_w_o,
              lru_w_in, lru_conv_w, lru_conv_b, lru_w_a, lru_b_a, lru_w_i, lru_b_i, lru_lam, lru_w_out,
              ffn_w_gate, ffn_w_up, ffn_w_down):
    xp, xs = x_prompt, x_sample
    kp_l, vp_l, lp_l, ks_l, vs_l, ls_l = [], [], [], [], [], []
    cp_l, hp_l, cs_l, hs_l = [], [], [], []
    for i in range(DEPTH):
        j = i // N_MIXERS
        hp = rms_norm(xp, norm_mix_pre[i])
        hs = rms_norm(xs, norm_mix_pre[i])
        if i % N_MIXERS == 0:
            yp, kp, vp, lp = fox_prompt(hp, fox_w_qkv[j], fox_w_f[j], fox_b_f[j], fox_w_o[j])
            ys, ks_, vs_, ls_ = fox_sample(hs, cache_k[j], cache_v[j], cache_logf[j], page_table,
                                           fox_w_qkv[j], fox_w_f[j], fox_b_f[j], fox_w_o[j])
            kp_l.append(kp); vp_l.append(vp); lp_l.append(lp)
            ks_l.append(ks_); vs_l.append(vs_); ls_l.append(ls_)
        else:
            lru_args = (lru_w_in[j], lru_conv_w[j], lru_conv_b[j], lru_w_a[j], lru_b_a[j],
                        lru_w_i[j], lru_b_i[j], lru_lam[j], lru_w_out[j])
            zero_buf = jnp.zeros((xp.shape[0], CONV_W - 1, D_RNN), hp.dtype)
            zero_h = jnp.zeros((xp.shape[0], D_RNN), jnp.float32)
            yp, cbp, hlp = rglru_mixer(hp, zero_buf, zero_h, *lru_args)
            ys, cbs, hls = rglru_mixer(hs, state_conv[j], state_h[j], *lru_args)
            cp_l.append(cbp); hp_l.append(hlp); cs_l.append(cbs); hs_l.append(hls)
        xp = xp + rms_norm(yp, norm_mix_post[i])
        xs = xs + rms_norm(ys, norm_mix_post[i])
        xp = xp + rms_norm(swiglu(rms_norm(xp, norm_ffn_pre[i]), ffn_w_gate[i], ffn_w_up[i], ffn_w_down[i]), norm_ffn_post[i])
        xs = xs + rms_norm(swiglu(rms_norm(xs, norm_ffn_pre[i]), ffn_w_gate[i], ffn_w_up[i], ffn_w_down[i]), norm_ffn_post[i])
    new_k_prompt = jnp.stack(kp_l)
    new_v_prompt = jnp.stack(vp_l)
    new_logf_prompt = jnp.stack(lp_l)
    new_k_sample = jnp.stack(ks_l)
    new_v_sample = jnp.stack(vs_l)
    new_logf_sample = jnp.stack(ls_l)
    new_conv_prompt = jnp.stack(cp_l)
    new_h_prompt = jnp.stack(hp_l)
    new_conv_sample = jnp.stack(cs_l)
    new_h_sample = jnp.stack(hs_l)
    return (xp, xs, new_k_prompt, new_v_prompt, new_logf_prompt, new_k_sample, new_v_sample,
            new_logf_sample, new_conv_prompt, new_h_prompt, new_conv_sample, new_h_sample)
```

```python
import functools
import math

import jax
import jax.numpy as jnp
from jax import lax
from jax.experimental import pallas as pl
from jax.experimental.pallas import tpu as pltpu

F32 = jnp.float32
BF16 = jnp.bfloat16

RMS_EPS = 1e-6
NEG_INF = -1e30
HEAD_DIM = 128
C_RG = 8.0
CONV_W = 4

LANES = 128
SUBLANES = 8
MXU_COLS = 256
VMEM_LIMIT_BYTES = 56 * 1024 * 1024

SAMPLE_ROWS = SUBLANES
GATE_TILE = MXU_COLS


def _params(*semantics):
    return pltpu.CompilerParams(dimension_semantics=semantics, vmem_limit_bytes=VMEM_LIMIT_BYTES)


def _nt_dot(a, b):
    return lax.dot_general(a, b, (((1,), (1,)), ((), ())), preferred_element_type=F32)


def _exact_dot(x, u):
    hi = x.astype(BF16)
    r1 = x - hi.astype(F32)
    mid = r1.astype(BF16)
    lo = (r1 - mid.astype(F32)).astype(BF16)
    dot = functools.partial(jnp.dot, preferred_element_type=F32)
    return dot(hi, u) + dot(mid, u) + dot(lo, u)


def _tri(n, strict_lower):
    rp = lax.broadcasted_iota(jnp.int32, (n, n), 0)
    r = lax.broadcasted_iota(jnp.int32, (n, n), 1)
    return jnp.where(rp > r if strict_lower else rp <= r, 1.0, 0.0).astype(BF16)


def _log_sigmoid(z):
    return -(jnp.maximum(-z, 0.0) + jnp.log1p(jnp.exp(-jnp.abs(z))))


def _softplus(z):
    return jnp.maximum(z, 0.0) + jnp.log1p(jnp.exp(-jnp.abs(z)))


def _gelu_tanh(x):
    cdf = 0.5 * (1.0 + jnp.tanh(math.sqrt(2.0 / math.pi) * (x + 0.044715 * (x * x * x))))
    return x * cdf


def _epi_store(accs, o_ref):
    o_ref[...] = accs[0].astype(o_ref.dtype)


def _epi_store_both(accs, o32_ref, o16_ref):
    o32_ref[...] = accs[0]
    o16_ref[...] = accs[0].astype(BF16)


def _epi_swiglu(accs, o_ref):
    g, u = accs
    o_ref[...] = ((g * jax.nn.sigmoid(g)) * u).astype(o_ref.dtype)


def _epi_log_sigmoid(accs, b_ref, o_ref):
    o_ref[...] = _log_sigmoid(accs[0] + b_ref[...])


def _norm_mm_body(x_ref, g_ref, *refs, n_w, epilogue):
    w_refs, rest, xn_ref = refs[:n_w], refs[n_w:-1], refs[-1]

    @pl.when(pl.program_id(1) == 0)
    def _():
        x = x_ref[...]
        var = jnp.mean(x * x, axis=-1, keepdims=True)
        xn_ref[...] = ((x * lax.rsqrt(var + RMS_EPS)) * g_ref[...]).astype(BF16)

    xn = xn_ref[...]
    accs = [jnp.dot(xn, w[...], preferred_element_type=F32) for w in w_refs]
    epilogue(accs, *rest)


def _col_block(i, j, *, off):
    return (0, j + off)


def _norm_matmul(x, g, ws, col_offs, n_cols, tn, epilogue, out_dtypes, bias=None, tm=512):
    m, k = x.shape
    tm = min(tm, m)
    in_specs = [
        pl.BlockSpec((tm, k), lambda i, j: (i, 0)),
        pl.BlockSpec((1, k), lambda i, j: (0, 0)),
    ]
    for off in col_offs:
        assert off % tn == 0
        in_specs.append(pl.BlockSpec((k, tn), functools.partial(_col_block, off=off // tn)))
    args = [x, g.reshape(1, k), *ws]
    if bias is not None:
        in_specs.append(pl.BlockSpec((1, tn), lambda i, j: (0, j)))
        args.append(bias)
    outs = pl.pallas_call(
        functools.partial(_norm_mm_body, n_w=len(ws), epilogue=epilogue),
        grid=(m // tm, n_cols // tn),
        in_specs=in_specs,
        out_specs=[pl.BlockSpec((tm, tn), lambda i, j: (i, j)) for _ in out_dtypes],
        out_shape=[jax.ShapeDtypeStruct((m, n_cols), dt) for dt in out_dtypes],
        scratch_shapes=[pltpu.VMEM((tm, k), BF16)],
        compiler_params=_params("parallel", "arbitrary"),
        name="norm_mm_" + epilogue.__name__.removeprefix("_epi_"),
    )(*args)
    return outs


def _mm_post_body(a_ref, w_ref, g_ref, res_ref, o_ref, y_ref, *, n_j, tn, d):
    j = pl.program_id(1)
    y_ref[j] = jnp.dot(a_ref[...], w_ref[...], preferred_element_type=F32)

    @pl.when(j == n_j - 1)
    def _():
        ss = None
        for jj in range(n_j):
            y = y_ref[jj]
            part = jnp.sum(y * y, axis=-1, keepdims=True)
            ss = part if ss is None else ss + part
        inv = lax.rsqrt(ss / d + RMS_EPS)
        for jj in range(n_j):
            sl = slice(jj * tn, (jj + 1) * tn)
            o_ref[:, sl] = res_ref[:, sl] + (y_ref[jj] * inv) * g_ref[:, sl]


def _matmul_post_norm(a, w, g, res, tn=512, tm=512):
    m, k = a.shape
    d = w.shape[1]
    tm = min(tm, m)
    n_j = d // tn
    return pl.pallas_call(
        functools.partial(_mm_post_body, n_j=n_j, tn=tn, d=d),
        grid=(m // tm, n_j),
        in_specs=[
            pl.BlockSpec((tm, k), lambda i, j: (i, 0)),
            pl.BlockSpec((k, tn), lambda i, j: (0, j)),
            pl.BlockSpec((1, d), lambda i, j: (0, 0)),
            pl.BlockSpec((tm, d), lambda i, j: (i, 0)),
        ],
        out_specs=pl.BlockSpec((tm, d), lambda i, j: (i, 0)),
        out_shape=jax.ShapeDtypeStruct((m, d), F32),
        scratch_shapes=[pltpu.VMEM((n_j, tm, tn), F32)],
        compiler_params=_params("parallel", "arbitrary"),
        name="mm_post_norm",
    )(a, w, g.reshape(1, d), res)


def _cumsum_body(x_ref, o_ref):
    rows, n = x_ref.shape
    u = _tri(LANES, strict_lower=False)
    carry = jnp.zeros((rows, 1), F32)
    for c in range(n // LANES):
        sl = slice(c * LANES, (c + 1) * LANES)
        cs = _exact_dot(x_ref[:, sl], u) + carry
        o_ref[:, sl] = cs
        carry = cs[:, LANES - 1 :]


def _cumsum_lanes(x):
    return pl.pallas_call(
        _cumsum_body,
        out_shape=jax.ShapeDtypeStruct(x.shape, F32),
        name="cumsum_lanes",
    )(x)


def _flash_body(q_ref, k_ref, v_ref, c_ref, o_ref, m_ref, l_ref, acc_ref, *, tq, scale):
    qi = pl.program_id(2)
    q = q_ref[0]
    m_ref[...] = jnp.full(m_ref.shape, NEG_INF, F32)
    l_ref[...] = jnp.zeros(l_ref.shape, F32)
    acc_ref[...] = jnp.zeros(acc_ref.shape, F32)

    def tile(j, masked):
        start = pl.multiple_of(j * tq, tq)
        k = k_ref[0, pl.ds(start, tq), :]
        v = v_ref[0, pl.ds(start, tq), :]
        s = _nt_dot(q, k) * scale - c_ref[0, 0, pl.ds(j, 1), :]
        if masked:
            row = lax.broadcasted_iota(jnp.int32, (tq, tq), 0)
            col = lax.broadcasted_iota(jnp.int32, (tq, tq), 1)
            s = jnp.where(col <= row, s, NEG_INF)
        m_prev = m_ref[...]
        m_new = jnp.maximum(m_prev, jnp.max(s, axis=-1, keepdims=True))
        alpha = jnp.exp(m_prev - m_new)
        p = jnp.exp(s - m_new)
        l_ref[...] = alpha * l_ref[...] + jnp.sum(p, axis=-1, keepdims=True)
        acc_ref[...] = alpha * acc_ref[...] + jnp.dot(p.astype(BF16), v, preferred_element_type=F32)
        m_ref[...] = m_new

    def body(j, carry):
        tile(j, False)
        return carry

    lax.fori_loop(0, qi, body, 0)
    tile(qi, True)
    o_ref[0] = (acc_ref[...] / l_ref[...]).astype(o_ref.dtype)


def _fox_prompt_attention(q, k, v, c, tq=512):
    b, s, d = q.shape
    h = d // HEAD_DIM
    n_q = s // tq
    c4 = c.reshape(b, h, n_q, tq)
    return pl.pallas_call(
        functools.partial(_flash_body, tq=tq, scale=1.0 / math.sqrt(HEAD_DIM)),
        grid=(b, h, n_q),
        in_specs=[
            pl.BlockSpec((1, tq, HEAD_DIM), lambda bi, hi, qi: (bi, qi, hi)),
            pl.BlockSpec((1, s, HEAD_DIM), lambda bi, hi, qi: (bi, 0, hi)),
            pl.BlockSpec((1, s, HEAD_DIM), lambda bi, hi, qi: (bi, 0, hi)),
            pl.BlockSpec((1, 1, n_q, tq), lambda bi, hi, qi: (bi, hi, 0, 0)),
        ],
        out_specs=pl.BlockSpec((1, tq, HEAD_DIM), lambda bi, hi, qi: (bi, qi, hi)),
        out_shape=jax.ShapeDtypeStruct((b, s, d), BF16),
        scratch_shapes=[
            pltpu.VMEM((tq, 1), F32),
            pltpu.VMEM((tq, 1), F32),
            pltpu.VMEM((tq, HEAD_DIM), F32),
        ],
        compiler_params=_params("parallel", "parallel", "arbitrary"),
        name="fox_prompt_attention",
    )(q, k, v, c4)


def _decode_body(pt_ref, q_ref, k_ref, v_ref, lf_ref, kn_ref, vn_ref, lfn_ref, o_ref,
                 qbd_ref, m_ref, l_ref, acc_ref, carry_ref, *, n_pages, n_tok, n_heads, scale):
    del pt_ref
    p = pl.program_id(1)
    d = n_heads * HEAD_DIM
    head_mask = (lax.broadcasted_iota(jnp.int32, (n_heads, d), 1) // HEAD_DIM
                 == lax.broadcasted_iota(jnp.int32, (n_heads, d), 0))

    @pl.when(p == 0)
    def _():
        q = q_ref[0].astype(F32)
        rows = [jnp.where(head_mask, jnp.broadcast_to(q[t : t + 1, :], (n_heads, d)), 0.0)
                for t in range(n_tok)]
        qbd_ref[...] = jnp.concatenate(rows, axis=0).astype(BF16)
        m_ref[...] = jnp.full(m_ref.shape, NEG_INF, F32)
        l_ref[...] = jnp.zeros(l_ref.shape, F32)
        acc_ref[...] = jnp.zeros(acc_ref.shape, F32)
        carry_ref[...] = jnp.zeros(carry_ref.shape, F32)

    def block(kb, vb, bias):
        s = _nt_dot(qbd_ref[...], kb) * scale + bias
        m_prev = m_ref[...]
        m_new = jnp.maximum(m_prev, jnp.max(s, axis=-1, keepdims=True))
        alpha = jnp.exp(m_prev - m_new)
        pr = jnp.exp(s - m_new)
        l_ref[...] = alpha * l_ref[...] + jnp.sum(pr, axis=-1, keepdims=True)
        acc_ref[...] = alpha * acc_ref[...] + jnp.dot(pr.astype(BF16), vb, preferred_element_type=F32)
        m_ref[...] = m_new

    lf = lf_ref[0]
    bias16 = _exact_dot(lf, _tri(LANES, strict_lower=True)) + carry_ref[...]
    carry_ref[...] = carry_ref[...] + jnp.sum(lf, axis=-1, keepdims=True)
    block(k_ref[0].astype(BF16), v_ref[0].astype(BF16), jnp.concatenate([bias16] * n_tok, axis=0))

    @pl.when(p == n_pages - 1)
    def _():
        c_new = _exact_dot(lfn_ref[0], _tri(LANES, strict_lower=False))
        key = lax.broadcasted_iota(jnp.int32, (n_heads, LANES), 1)
        bias = jnp.concatenate([jnp.where(key <= t, -c_new, NEG_INF) for t in range(n_tok)], axis=0)
        block(kn_ref[0], vn_ref[0], bias)
        inv_l = 1.0 / l_ref[...]
        out = []
        for t in range(n_tok):
            rows = slice(t * n_heads, (t + 1) * n_heads)
            blk = acc_ref[rows, :] * inv_l[rows]
            out.append(jnp.sum(jnp.where(head_mask, blk, 0.0), axis=0, keepdims=True))
        out.append(jnp.zeros((o_ref.shape[1] - n_tok, d), F32))
        o_ref[0] = jnp.concatenate(out, axis=0).astype(o_ref.dtype)


def _fox_sample_attention(page_table, q, k_pages, v_pages, lf_pages_t, k_new, v_new, lf_new_t,
                          base, n_tok):
    b, rows, d = q.shape
    n_heads = d // HEAD_DIM
    n_pages = page_table.shape[1]
    page = k_pages.shape[1]

    def page_block(bi, pi, pt):
        return (base + pt[bi, n_pages - 1 - pi], 0, 0)

    def per_seq(bi, pi, pt):
        return (bi, 0, 0)

    grid_spec = pltpu.PrefetchScalarGridSpec(
        num_scalar_prefetch=1,
        grid=(b, n_pages),
        in_specs=[
            pl.BlockSpec((1, rows, d), per_seq),
            pl.BlockSpec((1, page, d), page_block),
            pl.BlockSpec((1, page, d), page_block),
            pl.BlockSpec((1, n_heads, page), page_block),
            pl.BlockSpec((1, page, d), per_seq),
            pl.BlockSpec((1, page, d), per_seq),
            pl.BlockSpec((1, n_heads, page), per_seq),
        ],
        out_specs=pl.BlockSpec((1, rows, d), per_seq),
        scratch_shapes=[
            pltpu.VMEM((n_tok * n_heads, d), BF16),
            pltpu.VMEM((n_tok * n_heads, 1), F32),
            pltpu.VMEM((n_tok * n_heads, 1), F32),
            pltpu.VMEM((n_tok * n_heads, d), F32),
            pltpu.VMEM((n_heads, 1), F32),
        ],
    )
    return pl.pallas_call(
        functools.partial(_decode_body, n_pages=n_pages, n_tok=n_tok, n_heads=n_heads,
                          scale=1.0 / math.sqrt(HEAD_DIM)),
        grid_spec=grid_spec,
        out_shape=jax.ShapeDtypeStruct((b, rows, d), BF16),
        compiler_params=_params("parallel", "arbitrary"),
        name="fox_sample_attention",
    )(page_table, q, k_pages, v_pages, lf_pages_t, k_new, v_new, lf_new_t)


def _gate_windows(d_rnn, block_w):
    spans = []
    for c0 in range(0, d_rnn, GATE_TILE):
        n_lo, n_hi = c0 // block_w, (c0 + GATE_TILE - 1) // block_w
        spans.append((block_w * n_lo // LANES * LANES, block_w * (n_hi + 1)))
    win = max(-(-(hi - lo) // LANES) * LANES for lo, hi in spans)
    starts = [min(lo, d_rnn - win) for lo, _ in spans]
    assert all(s + win >= hi for s, (_, hi) in zip(starts, spans))
    return starts, win


def _pack_block_diag(w, starts, win):
    n_blocks, bw, _ = w.shape
    tiles = []
    for c, k0 in enumerate(starts):
        c0, c1 = c * GATE_TILE, (c + 1) * GATE_TILE
        tile = jnp.zeros((win, GATE_TILE), w.dtype)
        for n in range(c0 // bw, (c1 - 1) // bw + 1):
            g0, g1 = max(bw * n, c0), min(bw * (n + 1), c1)
            piece = w[n][:, g0 - bw * n : g1 - bw * n]
            r0 = bw * n - k0
            tile = tile + jnp.pad(piece, ((r0, win - r0 - bw), (g0 - c0, c1 - g1)))
        tiles.append(tile)
    return jnp.stack(tiles)


def _lru_body(rec_ref, gate_ref, cinit_ref, hinit_ref, cw_ref, cb_ref, wa_ref, wi_ref, ba_ref, bi_ref,
              lam_ref, y_ref, hlast_ref, prev_ref, h_ref, *, tm, n_valid, starts, win):
    @pl.when(pl.program_id(1) == 0)
    def _():
        prev_ref[...] = cinit_ref[0]
        h_ref[...] = hinit_ref[0]

    x = rec_ref[0]
    d_rnn = x.shape[1]
    prev = prev_ref[...]
    row8 = lax.broadcasted_iota(jnp.int32, (SUBLANES, 1), 0)
    row = lax.broadcasted_iota(jnp.int32, (tm, 1), 0)

    def delayed(k):
        r = pltpu.roll(x, k, axis=0)
        head = jnp.where(row8 < k, pltpu.roll(prev, k, axis=0), r[:SUBLANES])
        return head if tm == SUBLANES else jnp.concatenate([head, r[SUBLANES:]], axis=0)

    conv = cb_ref[...] + delayed(3) * cw_ref[0:1, :]
    conv = conv + delayed(2) * cw_ref[1:2, :]
    conv = conv + delayed(1) * cw_ref[2:3, :]
    conv = conv + x * cw_ref[3:4, :]
    prev_ref[...] = x[tm - SUBLANES :, :]
    conv_bf = conv.astype(BF16)
    sp = _softplus(-lam_ref[...])
    h_prev = h_ref[...]

    h_last = []
    for c, k0 in enumerate(starts):
        sl = slice(c * GATE_TILE, (c + 1) * GATE_TILE)
        window = conv_bf[:, k0 : k0 + win]
        r = jax.nn.sigmoid(jnp.dot(window, wa_ref[c], preferred_element_type=F32) + ba_ref[:, sl])
        ig = jax.nn.sigmoid(jnp.dot(window, wi_ref[c], preferred_element_type=F32) + bi_ref[:, sl])
        log_a = (-C_RG * r) * sp[:, sl]
        a = jnp.exp(log_a)
        xin = (jnp.sqrt(-(jnp.tanh(log_a) * (a * a + 1.0))) * ig) * conv[:, sl]
        if n_valid < tm:
            a = jnp.where(row < n_valid, a, 1.0)
            xin = jnp.where(row < n_valid, xin, 0.0)
        shift = 1
        while shift < tm:
            a_sh = jnp.where(row < shift, 1.0, pltpu.roll(a, shift, axis=0))
            x_sh = jnp.where(row < shift, 0.0, pltpu.roll(xin, shift, axis=0))
            xin = a * x_sh + xin
            a = a * a_sh
            shift *= 2
        h = a * h_prev[0:1, sl] + xin
        y_ref[0, :, sl] = (h * _gelu_tanh(gate_ref[0, :, sl])).astype(y_ref.dtype)
        h_last.append(h[tm - 1 :, :])
    h_new = jnp.broadcast_to(jnp.concatenate(h_last, axis=1), (SUBLANES, d_rnn))
    h_ref[...] = h_new
    hlast_ref[0] = h_new


def _rglru(rec, gate, conv_init, h_init, conv_w, conv_b, wa_pack, wi_pack, b_a, b_i, lam,
           starts, win, tm, n_valid):
    b, t, r = rec.shape
    n_tiles = wa_pack.shape[0]

    def const2(bi, ti):
        return (0, 0)

    def const3(bi, ti):
        return (0, 0, 0)

    def per_seq(bi, ti):
        return (bi, 0, 0)

    def tile(bi, ti):
        return (bi, ti, 0)

    return pl.pallas_call(
        functools.partial(_lru_body, tm=tm, n_valid=n_valid, starts=tuple(starts), win=win),
        grid=(b, t // tm),
        in_specs=[
            pl.BlockSpec((1, tm, r), tile),
            pl.BlockSpec((1, tm, r), tile),
            pl.BlockSpec((1, SUBLANES, r), per_seq),
            pl.BlockSpec((1, SUBLANES, r), per_seq),
            pl.BlockSpec((SUBLANES, r), const2),
            pl.BlockSpec((1, r), const2),
            pl.BlockSpec((n_tiles, win, GATE_TILE), const3),
            pl.BlockSpec((n_tiles, win, GATE_TILE), const3),
            pl.BlockSpec((1, r), const2),
            pl.BlockSpec((1, r), const2),
            pl.BlockSpec((1, r), const2),
        ],
        out_specs=[
            pl.BlockSpec((1, tm, r), tile),
            pl.BlockSpec((1, SUBLANES, r), per_seq),
        ],
        out_shape=[
            jax.ShapeDtypeStruct((b, t, r), BF16),
            jax.ShapeDtypeStruct((b, SUBLANES, r), F32),
        ],
        scratch_shapes=[pltpu.VMEM((SUBLANES, r), F32), pltpu.VMEM((SUBLANES, r), F32)],
        compiler_params=_params("parallel", "arbitrary"),
        name="rglru",
    )(rec, gate, conv_init, h_init, conv_w, conv_b, wa_pack, wi_pack, b_a, b_i, lam)


def _fox_project(x, g, w_qkv, w_f, b_f, d):
    q, = _norm_matmul(x, g, [w_qkv], [0], d, 1024, _epi_store, [BF16])
    k32, k16 = _norm_matmul(x, g, [w_qkv], [d], d, 1024, _epi_store_both, [F32, BF16])
    v32, v16 = _norm_matmul(x, g, [w_qkv], [2 * d], d, 1024, _epi_store_both, [F32, BF16])
    lf, = _norm_matmul(x, g, [w_f], [0], LANES, LANES, _epi_log_sigmoid, [F32], bias=b_f)
    return q, k32, k16, v32, v16, lf


def _ffn(x, g_pre, g_post, w_gate, w_up, w_down):
    d_ff = w_gate.shape[1]
    h, = _norm_matmul(x, g_pre, [w_gate, w_up], [0, 0], d_ff, 512, _epi_swiglu, [BF16])
    return _matmul_post_norm(h, w_down, g_post, x)


def kernel(x_prompt, x_sample, cache_k, cache_v, cache_logf, state_conv, state_h, page_table,
           norm_mix_pre, norm_mix_post, norm_ffn_pre, norm_ffn_post,
           fox_w_qkv, fox_w_f, fox_b_f, fox_w_o,
           lru_w_in, lru_conv_w, lru_conv_b, lru_w_a, lru_b_a, lru_w_i, lru_b_i, lru_lam, lru_w_out,
           ffn_w_gate, ffn_w_up, ffn_w_down):
    batch, seq, d = x_prompt.shape
    dec_batch, dec_seq, _ = x_sample.shape
    depth = norm_mix_pre.shape[0]
    n_heads = d // HEAD_DIM
    n_fox, n_pool, page = cache_logf.shape[:3]
    d_rnn = lru_lam.shape[1]
    block_w = lru_w_a.shape[2]
    m_p = batch * seq
    m_s = dec_batch * SAMPLE_ROWS
    pad_rows = SAMPLE_ROWS - dec_seq

    xp = x_prompt.reshape(m_p, d)
    xs = jnp.pad(x_sample, ((0, 0), (0, pad_rows), (0, 0))).reshape(m_s, d)

    k_pages = cache_k.reshape(n_fox * n_pool, page, d)
    v_pages = cache_v.reshape(n_fox * n_pool, page, d)
    lf_pages_t = cache_logf.transpose(0, 1, 3, 2).reshape(n_fox * n_pool, n_heads, page)
    starts, win = _gate_windows(d_rnn, block_w)

    kp_l, vp_l, lp_l, ks_l, vs_l, ls_l = [], [], [], [], [], []
    cp_l, hp_l, cs_l, hs_l = [], [], [], []
    for i in range(depth):
        j = i // 2
        g_pre = norm_mix_pre[i]
        if i % 2 == 0:
            w_qkv = fox_w_qkv[j].astype(BF16)
            w_f = jnp.pad(fox_w_f[j], ((0, 0), (0, LANES - n_heads))).astype(BF16)
            b_f = jnp.pad(fox_b_f[j], (0, LANES - n_heads)).reshape(1, LANES)
            w_o = fox_w_o[j].astype(BF16)

            q, k32, k16, v32, v16, lf = _fox_project(xp, g_pre, w_qkv, w_f, b_f, d)
            lf = lf[:, :n_heads].reshape(batch, seq, n_heads)
            c = _cumsum_lanes(lf.transpose(0, 2, 1).reshape(batch * n_heads, seq))
            o = _fox_prompt_attention(q.reshape(batch, seq, d), k16.reshape(batch, seq, d),
                                      v16.reshape(batch, seq, d), c.reshape(batch, n_heads, seq))
            xp = _matmul_post_norm(o.reshape(m_p, d), w_o, norm_mix_post[i], xp)
            kp_l.append(k32.reshape(batch, seq, n_heads, HEAD_DIM))
            vp_l.append(v32.reshape(batch, seq, n_heads, HEAD_DIM))
            lp_l.append(lf)

            q, k32, k16, v32, v16, lf = _fox_project(xs, g_pre, w_qkv, w_f, b_f, d)
            lf = lf[:, :n_heads].reshape(dec_batch, SAMPLE_ROWS, n_heads)[:, :dec_seq]
            lf_new_t = jnp.pad(lf.transpose(0, 2, 1), ((0, 0), (0, 0), (0, page - dec_seq)))
            pad_page = ((0, 0), (0, page - SAMPLE_ROWS), (0, 0))
            o = _fox_sample_attention(
                page_table, q.reshape(dec_batch, SAMPLE_ROWS, d), k_pages, v_pages, lf_pages_t,
                jnp.pad(k16.reshape(dec_batch, SAMPLE_ROWS, d), pad_page),
                jnp.pad(v16.reshape(dec_batch, SAMPLE_ROWS, d), pad_page),
                lf_new_t, j * n_pool, dec_seq)
            xs = _matmul_post_norm(o.reshape(m_s, d), w_o, norm_mix_post[i], xs)
            ks_l.append(k32.reshape(dec_batch, SAMPLE_ROWS, n_heads, HEAD_DIM)[:, :dec_seq])
            vs_l.append(v32.reshape(dec_batch, SAMPLE_ROWS, n_heads, HEAD_DIM)[:, :dec_seq])
            ls_l.append(lf)
        else:
            w_in = lru_w_in[j].astype(BF16)
            w_out = lru_w_out[j].astype(BF16)
            conv_w = jnp.pad(lru_conv_w[j], ((0, SUBLANES - CONV_W), (0, 0)))
            conv_b = lru_conv_b[j].reshape(1, d_rnn)
            wa_pack = _pack_block_diag(lru_w_a[j], starts, win).astype(BF16)
            wi_pack = _pack_block_diag(lru_w_i[j], starts, win).astype(BF16)
            b_a = lru_b_a[j].reshape(1, d_rnn)
            b_i = lru_b_i[j].reshape(1, d_rnn)
            lam = lru_lam[j].reshape(1, d_rnn)
            tn = d_rnn // 2

            def mixer(x, n_seq, t, conv_init, h_init, tm, n_valid):
                gate, = _norm_matmul(x, g_pre, [w_in], [0], d_rnn, tn, _epi_store, [F32])
                rec, = _norm_matmul(x, g_pre, [w_in], [d_rnn], d_rnn, tn, _epi_store, [F32])
                rec = rec.reshape(n_seq, t, d_rnn)
                y, h_last = _rglru(rec, gate.reshape(n_seq, t, d_rnn), conv_init, h_init, conv_w, conv_b,
                                   wa_pack, wi_pack, b_a, b_i, lam, starts, win, tm, n_valid)
                return y.reshape(n_seq * t, d_rnn), rec, h_last[:, 0]

            zeros = jnp.zeros((batch, SUBLANES, d_rnn), F32)
            y, rec, h_last = mixer(xp, batch, seq, zeros, zeros, 256, 256)
            xp = _matmul_post_norm(y, w_out, norm_mix_post[i], xp, tn=512)
            cp_l.append(rec[:, seq - (CONV_W - 1) :])
            hp_l.append(h_last)

            conv_init = jnp.pad(state_conv[j], ((0, 0), (SUBLANES - (CONV_W - 1), 0), (0, 0)))
            h_init = jnp.pad(state_h[j][:, None, :], ((0, 0), (0, SUBLANES - 1), (0, 0)))
            y, rec, h_last = mixer(xs, dec_batch, SAMPLE_ROWS, conv_init, h_init, SAMPLE_ROWS, dec_seq)
            xs = _matmul_post_norm(y, w_out, norm_mix_post[i], xs, tn=512)
            full = jnp.concatenate([state_conv[j], rec[:, :dec_seq]], axis=1)
            cs_l.append(full[:, dec_seq:])
            hs_l.append(h_last)

        w_gate = ffn_w_gate[i].astype(BF16)
        w_up = ffn_w_up[i].astype(BF16)
        w_down = ffn_w_down[i].astype(BF16)
        xp = _ffn(xp, norm_ffn_pre[i], norm_ffn_post[i], w_gate, w_up, w_down)
        xs = _ffn(xs, norm_ffn_pre[i], norm_ffn_post[i], w_gate, w_up, w_down)

    y_sample = xs.reshape(dec_batch, SAMPLE_ROWS, d)[:, :dec_seq]
    return (xp.reshape(batch, seq, d), y_sample,
            jnp.stack(kp_l), jnp.stack(vp_l), jnp.stack(lp_l),
            jnp.stack(ks_l), jnp.stack(vs_l), jnp.stack(ls_l),
            jnp.stack(cp_l), jnp.stack(hp_l), jnp.stack(cs_l), jnp.stack(hs_l))
```

```python
import functools
import math

import jax
import jax.numpy as jnp
from jax import lax
from jax.experimental import pallas as pl
from jax.experimental.pallas import tpu as pltpu

F32 = jnp.float32
BF16 = jnp.bfloat16

RMS_EPS = 1e-6
NEG_INF = -1e30
LOG2_E = math.log2(math.e)
HEAD_DIM = 128
C_RG = 8.0
CONV_W = 4

LANES = 128
SUBLANES = 8
MXU_COLS = 256
VMEM_LIMIT_BYTES = 56 * 1024 * 1024

SAMPLE_ROWS = SUBLANES
GATE_TILE = MXU_COLS


def _params(*semantics):
    return pltpu.CompilerParams(dimension_semantics=semantics, vmem_limit_bytes=VMEM_LIMIT_BYTES)


def _nt_dot(a, b):
    return lax.dot_general(a, b, (((1,), (1,)), ((), ())), preferred_element_type=F32)


def _exact_dot(x, u):
    hi = x.astype(BF16)
    r1 = x - hi.astype(F32)
    mid = r1.astype(BF16)
    lo = (r1 - mid.astype(F32)).astype(BF16)
    dot = functools.partial(jnp.dot, preferred_element_type=F32)
    return dot(hi, u) + dot(mid, u) + dot(lo, u)


def _tri(n, strict_lower):
    rp = lax.broadcasted_iota(jnp.int32, (n, n), 0)
    r = lax.broadcasted_iota(jnp.int32, (n, n), 1)
    return jnp.where(rp > r if strict_lower else rp <= r, 1.0, 0.0).astype(BF16)


def _log_sigmoid(z):
    return -(jnp.maximum(-z, 0.0) + jnp.log1p(jnp.exp(-jnp.abs(z))))


def _softplus(z):
    return jnp.maximum(z, 0.0) + jnp.log1p(jnp.exp(-jnp.abs(z)))


def _gelu_tanh(x):
    cdf = 0.5 * (1.0 + jnp.tanh(math.sqrt(2.0 / math.pi) * (x + 0.044715 * (x * x * x))))
    return x * cdf


def _epi_store(accs, o_ref):
    o_ref[...] = accs[0].astype(o_ref.dtype)


def _epi_store_both(accs, o32_ref, o16_ref):
    o32_ref[...] = accs[0]
    o16_ref[...] = accs[0].astype(BF16)


def _epi_swiglu(accs, o_ref):
    g, u = accs
    o_ref[...] = ((g * jax.nn.sigmoid(g)) * u).astype(o_ref.dtype)


def _epi_log_sigmoid(accs, b_ref, o_ref):
    o_ref[...] = _log_sigmoid(accs[0] + b_ref[...])


def _norm_mm_body(x_ref, g_ref, *refs, n_w, epilogue):
    w_refs, rest, xn_ref = refs[:n_w], refs[n_w:-1], refs[-1]

    @pl.when(pl.program_id(1) == 0)
    def _():
        x = x_ref[...]
        var = jnp.mean(x * x, axis=-1, keepdims=True)
        xn_ref[...] = ((x * lax.rsqrt(var + RMS_EPS)) * g_ref[...]).astype(BF16)

    xn = xn_ref[...]
    accs = [jnp.dot(xn, w[...], preferred_element_type=F32) for w in w_refs]
    epilogue(accs, *rest)


def _col_block(i, j, *, layer, off):
    return (layer, 0, j + off)


def _norm_matmul(x, g, ws, layer, col_offs, n_cols, tn, epilogue, out_dtypes, bias=None, tm=512):
    m, k = x.shape
    tm = min(tm, m)
    in_specs = [
        pl.BlockSpec((tm, k), lambda i, j: (i, 0)),
        pl.BlockSpec((1, k), lambda i, j: (0, 0)),
    ]
    for off in col_offs:
        assert off % tn == 0
        in_specs.append(pl.BlockSpec((None, k, tn),
                                     functools.partial(_col_block, layer=layer, off=off // tn)))
    args = [x, g.reshape(1, k), *ws]
    if bias is not None:
        in_specs.append(pl.BlockSpec((1, tn), lambda i, j: (0, j)))
        args.append(bias)
    outs = pl.pallas_call(
        functools.partial(_norm_mm_body, n_w=len(ws), epilogue=epilogue),
        grid=(m // tm, n_cols // tn),
        in_specs=in_specs,
        out_specs=[pl.BlockSpec((tm, tn), lambda i, j: (i, j)) for _ in out_dtypes],
        out_shape=[jax.ShapeDtypeStruct((m, n_cols), dt) for dt in out_dtypes],
        scratch_shapes=[pltpu.VMEM((tm, k), BF16)],
        compiler_params=_params("parallel", "arbitrary"),
        name="norm_mm_" + epilogue.__name__.removeprefix("_epi_"),
    )(*args)
    return outs


def _mm_post_body(a_ref, w_ref, g_ref, res_ref, o_ref, y_ref, *, n_j, tn, d):
    j = pl.program_id(1)
    y_ref[j] = jnp.dot(a_ref[...], w_ref[...], preferred_element_type=F32)

    @pl.when(j == n_j - 1)
    def _():
        ss = None
        for jj in range(n_j):
            y = y_ref[jj]
            part = jnp.sum(y * y, axis=-1, keepdims=True)
            ss = part if ss is None else ss + part
        inv = lax.rsqrt(ss / d + RMS_EPS)
        for jj in range(n_j):
            sl = slice(jj * tn, (jj + 1) * tn)
            o_ref[:, sl] = res_ref[:, sl] + (y_ref[jj] * inv) * g_ref[:, sl]


def _matmul_post_norm(a, w, layer, g, res, tn=512, tm=512):
    m, k = a.shape
    d = w.shape[2]
    tm = min(tm, m)
    n_j = d // tn
    return pl.pallas_call(
        functools.partial(_mm_post_body, n_j=n_j, tn=tn, d=d),
        grid=(m // tm, n_j),
        in_specs=[
            pl.BlockSpec((tm, k), lambda i, j: (i, 0)),
            pl.BlockSpec((None, k, tn), lambda i, j: (layer, 0, j)),
            pl.BlockSpec((1, d), lambda i, j: (0, 0)),
            pl.BlockSpec((tm, d), lambda i, j: (i, 0)),
        ],
        out_specs=pl.BlockSpec((tm, d), lambda i, j: (i, 0)),
        out_shape=jax.ShapeDtypeStruct((m, d), F32),
        scratch_shapes=[pltpu.VMEM((n_j, tm, tn), F32)],
        compiler_params=_params("parallel", "arbitrary"),
        name="mm_post_norm",
    )(a, w, g.reshape(1, d), res)


def _cumsum_body(x_ref, o_ref):
    rows, n = x_ref.shape
    u = _tri(LANES, strict_lower=False)
    carry = jnp.zeros((rows, 1), F32)
    for c in range(n // LANES):
        sl = slice(c * LANES, (c + 1) * LANES)
        cs = _exact_dot(x_ref[:, sl], u) + carry
        o_ref[:, sl] = cs
        carry = cs[:, LANES - 1 :]


def _cumsum_lanes(x):
    return pl.pallas_call(
        _cumsum_body,
        out_shape=jax.ShapeDtypeStruct(x.shape, F32),
        name="cumsum_lanes",
    )(x)


def _flash_body(q_ref, k_ref, v_ref, c_ref, o_ref, m_ref, l_ref, acc_ref, *, tq, n_hh, scale):
    qi = pl.program_id(2)
    m_ref[...] = jnp.full(m_ref.shape, NEG_INF, F32)
    l_ref[...] = jnp.zeros(l_ref.shape, F32)
    acc_ref[...] = jnp.zeros(acc_ref.shape, F32)
    reps = tq // LANES

    def tile(j, masked):
        start = pl.multiple_of(j * tq, tq)
        for hh in range(n_hh):
            cols = slice(hh * HEAD_DIM, (hh + 1) * HEAD_DIM)
            k = k_ref[0, pl.ds(start, tq), cols]
            v = v_ref[0, pl.ds(start, tq), cols]
            s = _nt_dot(q_ref[0, :, cols], k) * (scale * LOG2_E) - c_ref[0, hh, pl.ds(j, 1), :] * LOG2_E
            if masked:
                row = lax.broadcasted_iota(jnp.int32, (tq, tq), 0)
                col = lax.broadcasted_iota(jnp.int32, (tq, tq), 1)
                s = jnp.where(col <= row, s, NEG_INF)
            m_prev = m_ref[hh]
            m_new = jnp.maximum(m_prev, jnp.max(s, axis=-1, keepdims=True))
            alpha = jnp.exp2(m_prev - m_new)
            p = jnp.exp2(s - pltpu.repeat(m_new, reps, axis=1))
            l_ref[hh] = alpha * l_ref[hh] + jnp.sum(p, axis=-1, keepdims=True)
            acc_ref[hh] = alpha * acc_ref[hh] + jnp.dot(p.astype(BF16), v, preferred_element_type=F32)
            m_ref[hh] = m_new

    def body(j, carry):
        tile(j, False)
        return carry

    lax.fori_loop(0, qi, body, 0)
    tile(qi, True)
    for hh in range(n_hh):
        cols = slice(hh * HEAD_DIM, (hh + 1) * HEAD_DIM)
        o_ref[0, :, cols] = (acc_ref[hh] / l_ref[hh]).astype(o_ref.dtype)


def _fox_prompt_attention(q, k, v, c, tq=512, n_hh=2):
    b, s, d = q.shape
    h = d // HEAD_DIM
    n_q = s // tq
    c4 = c.reshape(b, h, n_q, tq)
    w = n_hh * HEAD_DIM
    return pl.pallas_call(
        functools.partial(_flash_body, tq=tq, n_hh=n_hh, scale=1.0 / math.sqrt(HEAD_DIM)),
        grid=(b, h // n_hh, n_q),
        in_specs=[
            pl.BlockSpec((1, tq, w), lambda bi, hi, qi: (bi, qi, hi)),
            pl.BlockSpec((1, s, w), lambda bi, hi, qi: (bi, 0, hi)),
            pl.BlockSpec((1, s, w), lambda bi, hi, qi: (bi, 0, hi)),
            pl.BlockSpec((1, n_hh, n_q, tq), lambda bi, hi, qi: (bi, hi, 0, 0)),
        ],
        out_specs=pl.BlockSpec((1, tq, w), lambda bi, hi, qi: (bi, qi, hi)),
        out_shape=jax.ShapeDtypeStruct((b, s, d), BF16),
        scratch_shapes=[
            pltpu.VMEM((n_hh, tq, LANES), F32),
            pltpu.VMEM((n_hh, tq, LANES), F32),
            pltpu.VMEM((n_hh, tq, HEAD_DIM), F32),
        ],
        compiler_params=_params("parallel", "parallel", "arbitrary"),
        name="fox_prompt_attention",
    )(q, k, v, c4)


def _strided_scan(x, lane, step, limit, reverse):
    n = x.shape[1]
    shift = step
    while shift < limit:
        if reverse:
            x = x + jnp.where(lane + shift < limit, pltpu.roll(x, n - shift, axis=1), 0.0)
        else:
            x = x + jnp.where(lane >= shift, pltpu.roll(x, shift, axis=1), 0.0)
        shift *= 2
    return x


def _decode_body(pt_ref, q_ref, *refs, n_pp, n_steps, n_tok, n_heads, scale):
    del pt_ref
    k_refs, v_refs = refs[:n_pp], refs[n_pp : 2 * n_pp]
    lf_refs, lf_next_refs = refs[2 * n_pp : 3 * n_pp], refs[3 * n_pp : 4 * n_pp]
    kn_ref, vn_ref, lfn_ref, o_ref, mask_ref, bias_ref, m_ref, l_ref, acc_ref, carry_ref = refs[4 * n_pp :]
    p = pl.program_id(1)
    n_rows = n_tok * n_heads
    n_keys = k_refs[0].shape[0]
    n_new = kn_ref.shape[1]
    lane = lax.broadcasted_iota(jnp.int32, (SUBLANES, n_keys), 1)

    def prepare_bias(lf_blocks):
        carry = carry_ref[...]
        for i, lf_ref in enumerate(lf_blocks):
            lf = jnp.broadcast_to(lf_ref[0], (SUBLANES, n_keys))
            later = jnp.where(lane + n_heads < n_keys, pltpu.roll(lf, n_keys - n_heads, axis=1), 0.0)
            bias_ref[i] = _strided_scan(later, lane, n_heads, n_keys, reverse=True) + carry
            total = lf
            shift = n_heads
            while shift < n_keys:
                total = total + pltpu.roll(total, shift, axis=1)
                shift *= 2
            carry = carry + total
        carry_ref[...] = carry

    @pl.when(p == 0)
    def _():
        row_h = lax.broadcasted_iota(jnp.int32, (n_rows, n_keys), 0) % n_heads
        key_h = lax.broadcasted_iota(jnp.int32, (n_rows, n_keys), 1) % n_heads
        mask_ref[...] = jnp.where(row_h == key_h, 0.0, NEG_INF)
        m_ref[...] = jnp.full(m_ref.shape, NEG_INF, F32)
        l_ref[...] = jnp.zeros(l_ref.shape, F32)
        acc_ref[...] = jnp.zeros(acc_ref.shape, F32)
        carry_ref[...] = jnp.zeros(carry_ref.shape, F32)
        prepare_bias(lf_refs)

    def blocks(kbs, vbs, biases):
        q = q_ref[0]
        ss = [_nt_dot(q, kb) * scale + bias for kb, bias in zip(kbs, biases)]
        m_prev = m_ref[...]
        m_new = m_prev
        for s in ss:
            m_new = jnp.maximum(m_new, jnp.max(s, axis=-1, keepdims=True))
        alpha = jnp.exp(m_prev - m_new)
        l_new = alpha * l_ref[...]
        acc = alpha * acc_ref[...]
        for s, vb in zip(ss, vbs):
            pr = jnp.exp(s - pltpu.repeat(m_new, s.shape[1] // LANES, axis=1))
            l_new = l_new + jnp.sum(pr, axis=-1, keepdims=True)
            acc = acc + jnp.dot(pr.astype(BF16), vb, preferred_element_type=F32)
        l_ref[...] = l_new
        acc_ref[...] = acc
        m_ref[...] = m_new

    blocks([k_ref[...].astype(BF16) for k_ref in k_refs],
           [v_ref[...].astype(BF16) for v_ref in v_refs],
           [mask_ref[...] + bias_ref[i, 0:1, :] for i in range(n_pp)])
    prepare_bias(lf_next_refs)

    @pl.when(p == n_steps - 1)
    def _():
        lane_n = lax.broadcasted_iota(jnp.int32, (SUBLANES, n_new), 1)
        c_new = _strided_scan(jnp.broadcast_to(lfn_ref[0], (SUBLANES, n_new)), lane_n, n_heads,
                              n_rows, reverse=False)
        row = lax.broadcasted_iota(jnp.int32, (n_rows, n_new), 0)
        key = lax.broadcasted_iota(jnp.int32, (n_rows, n_new), 1)
        ok = (row % n_heads == key % n_heads) & (key // n_heads <= row // n_heads)
        blocks([kn_ref[0]], [vn_ref[0]], [jnp.where(ok, -c_new[0:1, :], NEG_INF)])
        o_ref[0] = acc_ref[...] / l_ref[...]


def _fox_sample_attention(page_table, q, k_rows, v_rows, lf_pages, k_new, v_new, lf_new, base, n_heads,
                          n_pp=2):
    b, n_rows, hd = q.shape
    n_pages = page_table.shape[1]
    n_keys = lf_pages.shape[2]
    n_new = k_new.shape[1]
    assert n_pages % n_pp == 0
    n_steps = n_pages // n_pp

    def page_of(bi, pi, pt, slot):
        return base + pt[bi, jnp.maximum(n_pages - 1 - (pi * n_pp + slot), 0)]

    def page_rows(bi, pi, pt, *, slot):
        return (page_of(bi, pi, pt, slot), 0)

    def page_lf(bi, pi, pt, *, slot, ahead):
        return (page_of(bi, pi + ahead, pt, slot), 0, 0)

    def per_seq(bi, pi, pt):
        return (bi, 0, 0)

    slots = range(n_pp)
    rows_specs = [pl.BlockSpec((n_keys, hd), functools.partial(page_rows, slot=i)) for i in slots]
    lf_specs = [pl.BlockSpec((1, 1, n_keys), functools.partial(page_lf, slot=i, ahead=a))
                for a in (0, 1) for i in slots]
    grid_spec = pltpu.PrefetchScalarGridSpec(
        num_scalar_prefetch=1,
        grid=(b, n_steps),
        in_specs=[pl.BlockSpec((1, n_rows, hd), per_seq), *rows_specs, *rows_specs, *lf_specs,
                  pl.BlockSpec((1, n_new, hd), per_seq),
                  pl.BlockSpec((1, n_new, hd), per_seq),
                  pl.BlockSpec((1, 1, n_new), per_seq)],
        out_specs=pl.BlockSpec((1, n_rows, hd), per_seq),
        scratch_shapes=[
            pltpu.VMEM((n_rows, n_keys), F32),
            pltpu.VMEM((n_pp, SUBLANES, n_keys), F32),
            pltpu.VMEM((n_rows, LANES), F32),
            pltpu.VMEM((n_rows, LANES), F32),
            pltpu.VMEM((n_rows, hd), F32),
            pltpu.VMEM((SUBLANES, n_keys), F32),
        ],
    )
    return pl.pallas_call(
        functools.partial(_decode_body, n_pp=n_pp, n_steps=n_steps, n_tok=n_rows // n_heads,
                          n_heads=n_heads, scale=1.0 / math.sqrt(HEAD_DIM)),
        grid_spec=grid_spec,
        out_shape=jax.ShapeDtypeStruct((b, n_rows, hd), F32),
        compiler_params=_params("parallel", "arbitrary"),
        name="fox_sample_attention",
    )(page_table, q, *[k_rows] * n_pp, *[v_rows] * n_pp, *[lf_pages] * (2 * n_pp), k_new, v_new, lf_new)


def _gate_windows(d_rnn, block_w):
    spans = []
    for c0 in range(0, d_rnn, GATE_TILE):
        n_lo, n_hi = c0 // block_w, (c0 + GATE_TILE - 1) // block_w
        spans.append((block_w * n_lo // LANES * LANES, block_w * (n_hi + 1)))
    win = max(-(-(hi - lo) // LANES) * LANES for lo, hi in spans)
    starts = [min(lo, d_rnn - win) for lo, _ in spans]
    assert all(s + win >= hi for s, (_, hi) in zip(starts, spans))
    return starts, win


def _pack_block_diag(w, starts, win):
    n_blocks, bw, _ = w.shape
    tiles = []
    for c, k0 in enumerate(starts):
        c0, c1 = c * GATE_TILE, (c + 1) * GATE_TILE
        tile = jnp.zeros((win, GATE_TILE), w.dtype)
        for n in range(c0 // bw, (c1 - 1) // bw + 1):
            g0, g1 = max(bw * n, c0), min(bw * (n + 1), c1)
            piece = w[n][:, g0 - bw * n : g1 - bw * n]
            r0 = bw * n - k0
            tile = tile + jnp.pad(piece, ((r0, win - r0 - bw), (g0 - c0, c1 - g1)))
        tiles.append(tile)
    return jnp.stack(tiles)


def _lru_body(rec_ref, gate_ref, cinit_ref, hinit_ref, cw_ref, cb_ref, wa_ref, wi_ref, ba_ref, bi_ref,
              lam_ref, y_ref, hlast_ref, prev_ref, h_ref, *, tm, n_valid, starts, win):
    @pl.when(pl.program_id(1) == 0)
    def _():
        prev_ref[...] = cinit_ref[0]
        h_ref[...] = hinit_ref[0]

    x = rec_ref[0]
    d_rnn = x.shape[1]
    prev = prev_ref[...]
    row8 = lax.broadcasted_iota(jnp.int32, (SUBLANES, 1), 0)
    row = lax.broadcasted_iota(jnp.int32, (tm, 1), 0)

    def delayed(k):
        r = pltpu.roll(x, k, axis=0)
        head = jnp.where(row8 < k, pltpu.roll(prev, k, axis=0), r[:SUBLANES])
        return head if tm == SUBLANES else jnp.concatenate([head, r[SUBLANES:]], axis=0)

    conv = cb_ref[...] + delayed(3) * cw_ref[0:1, :]
    conv = conv + delayed(2) * cw_ref[1:2, :]
    conv = conv + delayed(1) * cw_ref[2:3, :]
    conv = conv + x * cw_ref[3:4, :]
    prev_ref[...] = x[tm - SUBLANES :, :]
    conv_bf = conv.astype(BF16)
    sp = _softplus(-lam_ref[...])
    h_prev = h_ref[...]

    h_last = []
    for c, k0 in enumerate(starts):
        sl = slice(c * GATE_TILE, (c + 1) * GATE_TILE)
        window = conv_bf[:, k0 : k0 + win]
        r = jax.nn.sigmoid(jnp.dot(window, wa_ref[c], preferred_element_type=F32) + ba_ref[:, sl])
        ig = jax.nn.sigmoid(jnp.dot(window, wi_ref[c], preferred_element_type=F32) + bi_ref[:, sl])
        log_a = (-C_RG * r) * sp[:, sl]
        a = jnp.exp(log_a)
        xin = (jnp.sqrt(-(jnp.tanh(log_a) * (a * a + 1.0))) * ig) * conv[:, sl]
        if n_valid < tm:
            a = jnp.where(row < n_valid, a, 1.0)
            xin = jnp.where(row < n_valid, xin, 0.0)
        shift = 1
        while shift < tm:
            a_sh = jnp.where(row < shift, 1.0, pltpu.roll(a, shift, axis=0))
            x_sh = jnp.where(row < shift, 0.0, pltpu.roll(xin, shift, axis=0))
            xin = a * x_sh + xin
            a = a * a_sh
            shift *= 2
        h = a * h_prev[0:1, sl] + xin
        y_ref[0, :, sl] = (h * _gelu_tanh(gate_ref[0, :, sl])).astype(y_ref.dtype)
        h_last.append(h[tm - 1 :, :])
    h_new = jnp.broadcast_to(jnp.concatenate(h_last, axis=1), (SUBLANES, d_rnn))
    h_ref[...] = h_new
    hlast_ref[0] = h_new


def _rglru(rec, gate, conv_init, h_init, conv_w, conv_b, wa_pack, wi_pack, b_a, b_i, lam,
           starts, win, tm, n_valid):
    b, t, r = rec.shape
    n_tiles = wa_pack.shape[0]

    def const2(bi, ti):
        return (0, 0)

    def const3(bi, ti):
        return (0, 0, 0)

    def per_seq(bi, ti):
        return (bi, 0, 0)

    def tile(bi, ti):
        return (bi, ti, 0)

    return pl.pallas_call(
        functools.partial(_lru_body, tm=tm, n_valid=n_valid, starts=tuple(starts), win=win),
        grid=(b, t // tm),
        in_specs=[
            pl.BlockSpec((1, tm, r), tile),
            pl.BlockSpec((1, tm, r), tile),
            pl.BlockSpec((1, SUBLANES, r), per_seq),
            pl.BlockSpec((1, SUBLANES, r), per_seq),
            pl.BlockSpec((SUBLANES, r), const2),
            pl.BlockSpec((1, r), const2),
            pl.BlockSpec((n_tiles, win, GATE_TILE), const3),
            pl.BlockSpec((n_tiles, win, GATE_TILE), const3),
            pl.BlockSpec((1, r), const2),
            pl.BlockSpec((1, r), const2),
            pl.BlockSpec((1, r), const2),
        ],
        out_specs=[
            pl.BlockSpec((1, tm, r), tile),
            pl.BlockSpec((1, SUBLANES, r), per_seq),
        ],
        out_shape=[
            jax.ShapeDtypeStruct((b, t, r), BF16),
            jax.ShapeDtypeStruct((b, SUBLANES, r), F32),
        ],
        scratch_shapes=[pltpu.VMEM((SUBLANES, r), F32), pltpu.VMEM((SUBLANES, r), F32)],
        compiler_params=_params("parallel", "arbitrary"),
        name="rglru",
    )(rec, gate, conv_init, h_init, conv_w, conv_b, wa_pack, wi_pack, b_a, b_i, lam)


def _fox_project(x, g, w_qkv, w_f, b_f, layer, d):
    q, = _norm_matmul(x, g, [w_qkv], layer, [0], d, 1024, _epi_store, [BF16])
    k32, k16 = _norm_matmul(x, g, [w_qkv], layer, [d], d, 1024, _epi_store_both, [F32, BF16])
    v32, v16 = _norm_matmul(x, g, [w_qkv], layer, [2 * d], d, 1024, _epi_store_both, [F32, BF16])
    lf, = _norm_matmul(x, g, [w_f], layer, [0], LANES, LANES, _epi_log_sigmoid, [F32], bias=b_f)
    return q, k32, k16, v32, v16, lf


def _ffn(x, g_pre, g_post, w_gate, w_up, w_down, layer):
    d_ff = w_gate.shape[2]
    h, = _norm_matmul(x, g_pre, [w_gate, w_up], layer, [0, 0], d_ff, 512, _epi_swiglu, [BF16])
    return _matmul_post_norm(h, w_down, layer, g_post, x)


def kernel(x_prompt, x_sample, cache_k, cache_v, cache_logf, state_conv, state_h, page_table,
           norm_mix_pre, norm_mix_post, norm_ffn_pre, norm_ffn_post,
           fox_w_qkv, fox_w_f, fox_b_f, fox_w_o,
           lru_w_in, lru_conv_w, lru_conv_b, lru_w_a, lru_b_a, lru_w_i, lru_b_i, lru_lam, lru_w_out,
           ffn_w_gate, ffn_w_up, ffn_w_down):
    batch, seq, d = x_prompt.shape
    dec_batch, dec_seq, _ = x_sample.shape
    depth = norm_mix_pre.shape[0]
    n_heads = d // HEAD_DIM
    n_fox, n_pool, page = cache_logf.shape[:3]
    d_rnn = lru_lam.shape[1]
    block_w = lru_w_a.shape[2]
    m_p = batch * seq
    m_s = dec_batch * SAMPLE_ROWS
    pad_rows = SAMPLE_ROWS - dec_seq

    xp = x_prompt.reshape(m_p, d)
    xs = jnp.pad(x_sample, ((0, 0), (0, pad_rows), (0, 0))).reshape(m_s, d)

    k_rows = cache_k.reshape(n_fox * n_pool * page * n_heads, HEAD_DIM)
    v_rows = cache_v.reshape(n_fox * n_pool * page * n_heads, HEAD_DIM)
    lf_pages = cache_logf.reshape(n_fox * n_pool, 1, page * n_heads)
    starts, win = _gate_windows(d_rnn, block_w)

    w_qkv = fox_w_qkv.astype(BF16)
    w_f = jnp.pad(fox_w_f, ((0, 0), (0, 0), (0, LANES - n_heads))).astype(BF16)
    w_o = fox_w_o.astype(BF16)
    w_in = lru_w_in.astype(BF16)
    w_out = lru_w_out.astype(BF16)
    w_gate = ffn_w_gate.astype(BF16)
    w_up = ffn_w_up.astype(BF16)
    w_down = ffn_w_down.astype(BF16)

    kp_l, vp_l, lp_l, ks_l, vs_l, ls_l = [], [], [], [], [], []
    cp_l, hp_l, cs_l, hs_l = [], [], [], []
    for i in range(depth):
        j = i // 2
        g_pre = norm_mix_pre[i]
        if i % 2 == 0:
            b_f = jnp.pad(fox_b_f[j], (0, LANES - n_heads)).reshape(1, LANES)

            q, k32, k16, v32, v16, lf = _fox_project(xp, g_pre, w_qkv, w_f, b_f, j, d)
            lf = lf[:, :n_heads].reshape(batch, seq, n_heads)
            c = _cumsum_lanes(lf.transpose(0, 2, 1).reshape(batch * n_heads, seq))
            o = _fox_prompt_attention(q.reshape(batch, seq, d), k16.reshape(batch, seq, d),
                                      v16.reshape(batch, seq, d), c.reshape(batch, n_heads, seq))
            xp = _matmul_post_norm(o.reshape(m_p, d), w_o, j, norm_mix_post[i], xp)
            kp_l.append(k32.reshape(batch, seq, n_heads, HEAD_DIM))
            vp_l.append(v32.reshape(batch, seq, n_heads, HEAD_DIM))
            lp_l.append(lf)

            q, k32, k16, v32, v16, lf = _fox_project(xs, g_pre, w_qkv, w_f, b_f, j, d)
            lf = lf[:, :n_heads].reshape(dec_batch, SAMPLE_ROWS, n_heads)[:, :dec_seq]
            lf_new = jnp.pad(lf, ((0, 0), (0, pad_rows), (0, 0))).reshape(dec_batch, 1, SAMPLE_ROWS * n_heads)
            q_rows = q.reshape(dec_batch, SAMPLE_ROWS, n_heads, HEAD_DIM)[:, :dec_seq]
            o = _fox_sample_attention(
                page_table, q_rows.reshape(dec_batch, dec_seq * n_heads, HEAD_DIM), k_rows, v_rows, lf_pages,
                k16.reshape(dec_batch, SAMPLE_ROWS * n_heads, HEAD_DIM),
                v16.reshape(dec_batch, SAMPLE_ROWS * n_heads, HEAD_DIM),
                lf_new, j * n_pool, n_heads)
            o = jnp.pad(o.reshape(dec_batch, dec_seq, d), ((0, 0), (0, pad_rows), (0, 0)))
            xs = _matmul_post_norm(o.reshape(m_s, d).astype(BF16), w_o, j, norm_mix_post[i], xs)
            ks_l.append(k32.reshape(dec_batch, SAMPLE_ROWS, n_heads, HEAD_DIM)[:, :dec_seq])
            vs_l.append(v32.reshape(dec_batch, SAMPLE_ROWS, n_heads, HEAD_DIM)[:, :dec_seq])
            ls_l.append(lf)
        else:
            conv_w = jnp.pad(lru_conv_w[j], ((0, SUBLANES - CONV_W), (0, 0)))
            conv_b = lru_conv_b[j].reshape(1, d_rnn)
            wa_pack = _pack_block_diag(lru_w_a[j], starts, win).astype(BF16)
            wi_pack = _pack_block_diag(lru_w_i[j], starts, win).astype(BF16)
            b_a = lru_b_a[j].reshape(1, d_rnn)
            b_i = lru_b_i[j].reshape(1, d_rnn)
            lam = lru_lam[j].reshape(1, d_rnn)
            tn = d_rnn // 2

            def mixer(x, n_seq, t, conv_init, h_init, tm, n_valid):
                gate, = _norm_matmul(x, g_pre, [w_in], j, [0], d_rnn, tn, _epi_store, [F32])
                rec, = _norm_matmul(x, g_pre, [w_in], j, [d_rnn], d_rnn, tn, _epi_store, [F32])
                rec = rec.reshape(n_seq, t, d_rnn)
                y, h_last = _rglru(rec, gate.reshape(n_seq, t, d_rnn), conv_init, h_init, conv_w, conv_b,
                                   wa_pack, wi_pack, b_a, b_i, lam, starts, win, tm, n_valid)
                return y.reshape(n_seq * t, d_rnn), rec, h_last[:, 0]

            zeros = jnp.zeros((batch, SUBLANES, d_rnn), F32)
            y, rec, h_last = mixer(xp, batch, seq, zeros, zeros, 256, 256)
            xp = _matmul_post_norm(y, w_out, j, norm_mix_post[i], xp)
            cp_l.append(rec[:, seq - (CONV_W - 1) :])
            hp_l.append(h_last)

            conv_init = jnp.pad(state_conv[j], ((0, 0), (SUBLANES - (CONV_W - 1), 0), (0, 0)))
            h_init = jnp.pad(state_h[j][:, None, :], ((0, 0), (0, SUBLANES - 1), (0, 0)))
            y, rec, h_last = mixer(xs, dec_batch, SAMPLE_ROWS, conv_init, h_init, SAMPLE_ROWS, dec_seq)
            xs = _matmul_post_norm(y, w_out, j, norm_mix_post[i], xs)
            full = jnp.concatenate([state_conv[j], rec[:, :dec_seq]], axis=1)
            cs_l.append(full[:, dec_seq:])
            hs_l.append(h_last)

        xp = _ffn(xp, norm_ffn_pre[i], norm_ffn_post[i], w_gate, w_up, w_down, i)
        xs = _ffn(xs, norm_ffn_pre[i], norm_ffn_post[i], w_gate, w_up, w_down, i)

    y_sample = xs.reshape(dec_batch, SAMPLE_ROWS, d)[:, :dec_seq]
    return (xp.reshape(batch, seq, d), y_sample,
            jnp.stack(kp_l), jnp.stack(vp_l), jnp.stack(lp_l),
            jnp.stack(ks_l), jnp.stack(vs_l), jnp.stack(ls_l),
            jnp.stack(cp_l), jnp.stack(hp_l), jnp.stack(cs_l), jnp.stack(hs_l))
```

```python
import functools
import math

import jax
import jax.numpy as jnp
from jax import lax
from jax.experimental import pallas as pl
from jax.experimental.pallas import tpu as pltpu

F32 = jnp.float32
BF16 = jnp.bfloat16

RMS_EPS = 1e-6
NEG_INF = -1e30
LOG2_E = math.log2(math.e)
HEAD_DIM = 128
C_RG = 8.0
CONV_W = 4

LANES = 128
SUBLANES = 8
MXU_COLS = 256
VMEM_LIMIT_BYTES = 56 * 1024 * 1024

SAMPLE_ROWS = SUBLANES
GATE_TILE = MXU_COLS


def _params(*semantics):
    return pltpu.CompilerParams(dimension_semantics=semantics, vmem_limit_bytes=VMEM_LIMIT_BYTES)


def _nt_dot(a, b):
    return lax.dot_general(a, b, (((1,), (1,)), ((), ())), preferred_element_type=F32)


def _exact_dot(x, u):
    hi = x.astype(BF16)
    r1 = x - hi.astype(F32)
    mid = r1.astype(BF16)
    lo = (r1 - mid.astype(F32)).astype(BF16)
    dot = functools.partial(jnp.dot, preferred_element_type=F32)
    return dot(hi, u) + dot(mid, u) + dot(lo, u)


def _tri(n, strict_lower):
    rp = lax.broadcasted_iota(jnp.int32, (n, n), 0)
    r = lax.broadcasted_iota(jnp.int32, (n, n), 1)
    return jnp.where(rp > r if strict_lower else rp <= r, 1.0, 0.0).astype(BF16)


def _log_sigmoid(z):
    return -(jnp.maximum(-z, 0.0) + jnp.log1p(jnp.exp(-jnp.abs(z))))


def _softplus(z):
    return jnp.maximum(z, 0.0) + jnp.log1p(jnp.exp(-jnp.abs(z)))


def _gelu_tanh(x):
    cdf = 0.5 * (1.0 + jnp.tanh(math.sqrt(2.0 / math.pi) * (x + 0.044715 * (x * x * x))))
    return x * cdf


NORM_CHUNK = 512


def _rms_norm_body(x_ref, g_ref, o_ref):
    x = x_ref[...]
    var = jnp.mean(x * x, axis=-1, keepdims=True)
    o_ref[...] = ((x * lax.rsqrt(var + RMS_EPS)) * g_ref[...]).astype(o_ref.dtype)


def _rms_norm(x, g, tm=512):
    m, d = x.shape
    tm = min(tm, m)
    return pl.pallas_call(
        _rms_norm_body,
        grid=(m // tm,),
        in_specs=[pl.BlockSpec((tm, d), lambda i: (i, 0)), pl.BlockSpec((1, d), lambda i: (0, 0))],
        out_specs=pl.BlockSpec((tm, d), lambda i: (i, 0)),
        out_shape=jax.ShapeDtypeStruct((m, d), BF16),
        compiler_params=_params("parallel"),
        name="rms_norm",
    )(x, g.reshape(1, d))


def _residual_norm_epilogue(y_chunk, res_ref, g_ref, gn_ref, o_ref, xn_ref, d):
    n_c = d // NORM_CHUNK
    chunks = [slice(c * NORM_CHUNK, (c + 1) * NORM_CHUNK) for c in range(n_c)]

    def sum_sq(get):
        total = None
        for c in range(n_c):
            v = get(c)
            part = jnp.sum(v * v, axis=-1, keepdims=True)
            total = part if total is None else total + part
        return total

    inv = lax.rsqrt(sum_sq(y_chunk) / d + RMS_EPS)
    for c, sl in enumerate(chunks):
        o_ref[:, sl] = res_ref[:, sl] + (y_chunk(c) * inv) * g_ref[:, sl]
    inv_n = lax.rsqrt(sum_sq(lambda c: o_ref[:, chunks[c]]) / d + RMS_EPS)
    for sl in chunks:
        xn_ref[:, sl] = ((o_ref[:, sl] * inv_n) * gn_ref[:, sl]).astype(xn_ref.dtype)


def _epi_store(acc, o_ref):
    o_ref[...] = acc.astype(o_ref.dtype)


def _epi_store_both(acc, o32_ref, o16_ref):
    o32_ref[...] = acc
    o16_ref[...] = acc.astype(BF16)


def _epi_log_sigmoid(acc, b_ref, o_ref):
    o_ref[...] = _log_sigmoid(acc + b_ref[...])


def _mm_body(x_ref, w_ref, *rest, epilogue):
    epilogue(jnp.dot(x_ref[...], w_ref[...], preferred_element_type=F32), *rest)


def _matmul(x, w, layer, col_off, n_cols, tn, epilogue, out_dtypes, bias=None, tm=1024):
    m, k = x.shape
    tm = min(tm, m)
    assert col_off % tn == 0
    off = col_off // tn
    in_specs = [
        pl.BlockSpec((tm, k), lambda i, j: (i, 0)),
        pl.BlockSpec((None, k, tn), lambda i, j: (layer, 0, j + off)),
    ]
    args = [x, w]
    if bias is not None:
        in_specs.append(pl.BlockSpec((1, tn), lambda i, j: (0, j)))
        args.append(bias)
    return pl.pallas_call(
        functools.partial(_mm_body, epilogue=epilogue),
        grid=(m // tm, n_cols // tn),
        in_specs=in_specs,
        out_specs=[pl.BlockSpec((tm, tn), lambda i, j: (i, j)) for _ in out_dtypes],
        out_shape=[jax.ShapeDtypeStruct((m, n_cols), dt) for dt in out_dtypes],
        compiler_params=_params("parallel", "arbitrary"),
        name="mm_" + epilogue.__name__.removeprefix("_epi_"),
    )(*args)


def _mm_post_body(a_ref, w_ref, g_ref, gn_ref, res_ref, o_ref, xn_ref, y_ref, *, n_j, d):
    j = pl.program_id(1)
    y_ref[j] = jnp.dot(a_ref[...], w_ref[...], preferred_element_type=F32)

    @pl.when(j == n_j - 1)
    def _():
        _residual_norm_epilogue(lambda c: y_ref[c], res_ref, g_ref, gn_ref, o_ref, xn_ref, d)


def _matmul_post_norm(a, w, layer, g, g_next, res, tm=512):
    m, k = a.shape
    d = w.shape[2]
    tm = min(tm, m)
    n_j = d // NORM_CHUNK
    row = pl.BlockSpec((1, d), lambda i, j: (0, 0))
    tile = pl.BlockSpec((tm, d), lambda i, j: (i, 0))
    return pl.pallas_call(
        functools.partial(_mm_post_body, n_j=n_j, d=d),
        grid=(m // tm, n_j),
        in_specs=[
            pl.BlockSpec((tm, k), lambda i, j: (i, 0)),
            pl.BlockSpec((None, k, NORM_CHUNK), lambda i, j: (layer, 0, j)),
            row, row, tile,
        ],
        out_specs=[tile, tile],
        out_shape=[jax.ShapeDtypeStruct((m, d), F32), jax.ShapeDtypeStruct((m, d), BF16)],
        scratch_shapes=[pltpu.VMEM((n_j, tm, NORM_CHUNK), F32)],
        compiler_params=_params("parallel", "arbitrary"),
        name="mm_post_norm",
    )(a, w, g.reshape(1, d), g_next.reshape(1, d), res)


def _ffn_body(xn_ref, x_ref, wg_ref, wu_ref, wd_ref, g_ref, gn_ref, o_ref, xn_out_ref, acc_ref, *, n_f, d):
    f = pl.program_id(1)

    @pl.when(f == 0)
    def _():
        acc_ref[...] = jnp.zeros(acc_ref.shape, F32)

    xn = xn_ref[...]
    gate = jnp.dot(xn, wg_ref[...], preferred_element_type=F32)
    up = jnp.dot(xn, wu_ref[...], preferred_element_type=F32)
    h = ((gate * jax.nn.sigmoid(gate)) * up).astype(BF16)
    for c in range(d // NORM_CHUNK):
        sl = slice(c * NORM_CHUNK, (c + 1) * NORM_CHUNK)
        acc_ref[:, sl] += jnp.dot(h, wd_ref[:, sl], preferred_element_type=F32)

    @pl.when(f == n_f - 1)
    def _():
        _residual_norm_epilogue(lambda c: acc_ref[:, c * NORM_CHUNK : (c + 1) * NORM_CHUNK],
                                x_ref, g_ref, gn_ref, o_ref, xn_out_ref, d)


def _ffn(xn, x, w_gate, w_up, w_down, layer, g, g_next, tm=512, tf=512):
    m, d = x.shape
    d_ff = w_gate.shape[2]
    tm = min(tm, m)
    n_f = d_ff // tf
    row = pl.BlockSpec((1, d), lambda i, f: (0, 0))
    tile = pl.BlockSpec((tm, d), lambda i, f: (i, 0))
    return pl.pallas_call(
        functools.partial(_ffn_body, n_f=n_f, d=d),
        grid=(m // tm, n_f),
        in_specs=[
            tile, tile,
            pl.BlockSpec((None, d, tf), lambda i, f: (layer, 0, f)),
            pl.BlockSpec((None, d, tf), lambda i, f: (layer, 0, f)),
            pl.BlockSpec((None, tf, d), lambda i, f: (layer, f, 0)),
            row, row,
        ],
        out_specs=[tile, tile],
        out_shape=[jax.ShapeDtypeStruct((m, d), F32), jax.ShapeDtypeStruct((m, d), BF16)],
        scratch_shapes=[pltpu.VMEM((tm, d), F32)],
        compiler_params=_params("parallel", "arbitrary"),
        name="ffn",
    )(xn, x, w_gate, w_up, w_down, g.reshape(1, d), g_next.reshape(1, d))


def _cumsum_body(x_ref, o_ref):
    rows, n = x_ref.shape
    u = _tri(LANES, strict_lower=False)
    carry = jnp.zeros((rows, 1), F32)
    for c in range(n // LANES):
        sl = slice(c * LANES, (c + 1) * LANES)
        cs = _exact_dot(x_ref[:, sl], u) + carry
        o_ref[:, sl] = cs
        carry = cs[:, LANES - 1 :]


def _cumsum_lanes(x):
    return pl.pallas_call(
        _cumsum_body,
        out_shape=jax.ShapeDtypeStruct(x.shape, F32),
        name="cumsum_lanes",
    )(x)


def _flash_body(q_ref, k_ref, v_ref, c_ref, o_ref, m_ref, l_ref, acc_ref, *, tq, n_hh, scale):
    qi = pl.program_id(2)
    m_ref[...] = jnp.full(m_ref.shape, NEG_INF, F32)
    l_ref[...] = jnp.zeros(l_ref.shape, F32)
    acc_ref[...] = jnp.zeros(acc_ref.shape, F32)
    reps = tq // LANES

    def tile(j, masked):
        start = pl.multiple_of(j * tq, tq)
        for hh in range(n_hh):
            cols = slice(hh * HEAD_DIM, (hh + 1) * HEAD_DIM)
            k = k_ref[0, pl.ds(start, tq), cols]
            v = v_ref[0, pl.ds(start, tq), cols]
            s = _nt_dot(q_ref[0, :, cols], k) * (scale * LOG2_E) - c_ref[0, hh, pl.ds(j, 1), :] * LOG2_E
            if masked:
                row = lax.broadcasted_iota(jnp.int32, (tq, tq), 0)
                col = lax.broadcasted_iota(jnp.int32, (tq, tq), 1)
                s = jnp.where(col <= row, s, NEG_INF)
            m_prev = m_ref[hh]
            m_new = jnp.maximum(m_prev, jnp.max(s, axis=-1, keepdims=True))
            alpha = jnp.exp2(m_prev - m_new)
            p = jnp.exp2(s - jnp.tile(m_new, (1, reps)))
            l_ref[hh] = alpha * l_ref[hh] + jnp.sum(p, axis=-1, keepdims=True)
            acc_ref[hh] = alpha * acc_ref[hh] + jnp.dot(p.astype(BF16), v, preferred_element_type=F32)
            m_ref[hh] = m_new

    def body(j, carry):
        tile(j, False)
        return carry

    lax.fori_loop(0, qi, body, 0)
    tile(qi, True)
    for hh in range(n_hh):
        cols = slice(hh * HEAD_DIM, (hh + 1) * HEAD_DIM)
        o_ref[0, :, cols] = (acc_ref[hh] / l_ref[hh]).astype(o_ref.dtype)


def _fox_prompt_attention(q, k, v, c, tq=512, n_hh=2):
    b, s, d = q.shape
    h = d // HEAD_DIM
    n_q = s // tq
    c4 = c.reshape(b, h, n_q, tq)
    w = n_hh * HEAD_DIM
    return pl.pallas_call(
        functools.partial(_flash_body, tq=tq, n_hh=n_hh, scale=1.0 / math.sqrt(HEAD_DIM)),
        grid=(b, h // n_hh, n_q),
        in_specs=[
            pl.BlockSpec((1, tq, w), lambda bi, hi, qi: (bi, qi, hi)),
            pl.BlockSpec((1, s, w), lambda bi, hi, qi: (bi, 0, hi)),
            pl.BlockSpec((1, s, w), lambda bi, hi, qi: (bi, 0, hi)),
            pl.BlockSpec((1, n_hh, n_q, tq), lambda bi, hi, qi: (bi, hi, 0, 0)),
        ],
        out_specs=pl.BlockSpec((1, tq, w), lambda bi, hi, qi: (bi, qi, hi)),
        out_shape=jax.ShapeDtypeStruct((b, s, d), BF16),
        scratch_shapes=[
            pltpu.VMEM((n_hh, tq, LANES), F32),
            pltpu.VMEM((n_hh, tq, LANES), F32),
            pltpu.VMEM((n_hh, tq, HEAD_DIM), F32),
        ],
        compiler_params=_params("parallel", "parallel", "arbitrary"),
        name="fox_prompt_attention",
    )(q, k, v, c4)


def _strided_scan(x, lane, step, limit, reverse):
    n = x.shape[1]
    shift = step
    while shift < limit:
        if reverse:
            x = x + jnp.where(lane + shift < limit, pltpu.roll(x, n - shift, axis=1), 0.0)
        else:
            x = x + jnp.where(lane >= shift, pltpu.roll(x, shift, axis=1), 0.0)
        shift *= 2
    return x


def _decode_body(pt_ref, q_ref, *refs, n_pp, n_groups, n_tok, n_heads, scale):
    del pt_ref
    k_refs, v_refs, lf_refs = refs[:n_pp], refs[n_pp : 2 * n_pp], refs[2 * n_pp : 3 * n_pp]
    kn_ref, vn_ref, lfn_ref, o_ref, mask_ref, bias_ref, m_ref, l_ref, acc_ref, carry_ref = refs[3 * n_pp :]
    p = pl.program_id(1)
    n_rows = n_tok * n_heads
    n_keys = k_refs[0].shape[0]
    n_new = kn_ref.shape[1]

    @pl.when(p == 0)
    def _():
        row_h = lax.broadcasted_iota(jnp.int32, (n_rows, n_keys), 0) % n_heads
        key_h = lax.broadcasted_iota(jnp.int32, (n_rows, n_keys), 1) % n_heads
        mask_ref[...] = jnp.where(row_h == key_h, 0.0, NEG_INF)
        bias_ref[...] = jnp.full(bias_ref.shape, NEG_INF, F32)
        m_ref[...] = jnp.full(m_ref.shape, 0.5 * NEG_INF, F32)
        l_ref[...] = jnp.zeros(l_ref.shape, F32)
        acc_ref[...] = jnp.zeros(acc_ref.shape, F32)
        carry_ref[...] = jnp.zeros(carry_ref.shape, F32)

    def blocks(kbs, vbs, biases):
        q = q_ref[0]
        ss = [_nt_dot(q, kb) * (scale * LOG2_E) + bias for kb, bias in zip(kbs, biases)]
        m_prev = m_ref[...]
        m_new = m_prev
        for s in ss:
            m_new = jnp.maximum(m_new, jnp.max(s, axis=-1, keepdims=True))
        alpha = jnp.exp2(m_prev - m_new)
        l_new = alpha * l_ref[...]
        acc = alpha * acc_ref[...]
        for s, vb in zip(ss, vbs):
            pr = jnp.exp2(s - jnp.tile(m_new, (1, s.shape[1] // LANES)))
            l_new = l_new + jnp.sum(pr, axis=-1, keepdims=True)
            acc = acc + jnp.dot(pr.astype(BF16), vb, preferred_element_type=F32)
        l_ref[...] = l_new
        acc_ref[...] = acc
        m_ref[...] = m_new

    blocks([k_ref[...].astype(BF16) for k_ref in k_refs],
           [v_ref[...].astype(BF16) for v_ref in v_refs],
           [mask_ref[...] + bias_ref[i, 0:1, :] for i in range(n_pp)])

    lane = lax.broadcasted_iota(jnp.int32, (SUBLANES, n_keys), 1)
    carry = carry_ref[...]
    for i, lf_ref in enumerate(lf_refs):
        lf = jnp.broadcast_to(lf_ref[0], (SUBLANES, n_keys)) * LOG2_E
        later = jnp.where(lane + n_heads < n_keys, pltpu.roll(lf, n_keys - n_heads, axis=1), 0.0)
        bias_ref[i] = _strided_scan(later, lane, n_heads, n_keys, reverse=True) + carry
        total = lf
        shift = n_heads
        while shift < n_keys:
            total = total + pltpu.roll(total, shift, axis=1)
            shift *= 2
        carry = carry + total
    carry_ref[...] = carry

    @pl.when(p == n_groups)
    def _():
        lane_n = lax.broadcasted_iota(jnp.int32, (SUBLANES, n_new), 1)
        c_new = _strided_scan(jnp.broadcast_to(lfn_ref[0], (SUBLANES, n_new)) * LOG2_E, lane_n, n_heads,
                              n_rows, reverse=False)
        row = lax.broadcasted_iota(jnp.int32, (n_rows, n_new), 0)
        key = lax.broadcasted_iota(jnp.int32, (n_rows, n_new), 1)
        ok = (row % n_heads == key % n_heads) & (key // n_heads <= row // n_heads)
        blocks([kn_ref[0]], [vn_ref[0]], [jnp.where(ok, -c_new[0:1, :], NEG_INF)])
        o_ref[0] = acc_ref[...] / l_ref[...]


def _fox_sample_attention(page_table, q, k_rows, v_rows, lf_pages, k_new, v_new, lf_new, base, n_heads,
                          n_pp=4):
    b, n_rows, hd = q.shape
    n_pages = page_table.shape[1]
    n_keys = lf_pages.shape[2]
    n_new = k_new.shape[1]
    assert n_pages % n_pp == 0
    n_groups = n_pages // n_pp

    def page_of(bi, group, pt, slot):
        return base + pt[bi, n_pages - 1 - (group * n_pp + slot)]

    def page_rows(bi, pi, pt, *, slot):
        return (page_of(bi, jnp.maximum(pi - 1, 0), pt, slot), 0)

    def page_lf(bi, pi, pt, *, slot):
        return (page_of(bi, jnp.minimum(pi, n_groups - 1), pt, slot), 0, 0)

    def per_seq(bi, pi, pt):
        return (bi, 0, 0)

    slots = range(n_pp)
    rows_specs = [pl.BlockSpec((n_keys, hd), functools.partial(page_rows, slot=i)) for i in slots]
    lf_specs = [pl.BlockSpec((1, 1, n_keys), functools.partial(page_lf, slot=i)) for i in slots]
    grid_spec = pltpu.PrefetchScalarGridSpec(
        num_scalar_prefetch=1,
        grid=(b, n_groups + 1),
        in_specs=[pl.BlockSpec((1, n_rows, hd), per_seq), *rows_specs, *rows_specs, *lf_specs,
                  pl.BlockSpec((1, n_new, hd), per_seq),
                  pl.BlockSpec((1, n_new, hd), per_seq),
                  pl.BlockSpec((1, 1, n_new), per_seq)],
        out_specs=pl.BlockSpec((1, n_rows, hd), per_seq),
        scratch_shapes=[
            pltpu.VMEM((n_rows, n_keys), F32),
            pltpu.VMEM((n_pp, SUBLANES, n_keys), F32),
            pltpu.VMEM((n_rows, LANES), F32),
            pltpu.VMEM((n_rows, LANES), F32),
            pltpu.VMEM((n_rows, hd), F32),
            pltpu.VMEM((SUBLANES, n_keys), F32),
        ],
    )
    return pl.pallas_call(
        functools.partial(_decode_body, n_pp=n_pp, n_groups=n_groups, n_tok=n_rows // n_heads,
                          n_heads=n_heads, scale=1.0 / math.sqrt(HEAD_DIM)),
        grid_spec=grid_spec,
        out_shape=jax.ShapeDtypeStruct((b, n_rows, hd), F32),
        compiler_params=_params("parallel", "arbitrary"),
        name="fox_sample_attention",
    )(page_table, q, *[k_rows] * n_pp, *[v_rows] * n_pp, *[lf_pages] * n_pp, k_new, v_new, lf_new)


def _gate_windows(d_rnn, block_w):
    spans = []
    for c0 in range(0, d_rnn, GATE_TILE):
        n_lo, n_hi = c0 // block_w, (c0 + GATE_TILE - 1) // block_w
        spans.append((block_w * n_lo // LANES * LANES, block_w * (n_hi + 1)))
    win = max(-(-(hi - lo) // LANES) * LANES for lo, hi in spans)
    starts = [min(lo, d_rnn - win) for lo, _ in spans]
    assert all(s + win >= hi for s, (_, hi) in zip(starts, spans))
    return starts, win


def _pack_block_diag(w, starts, win):
    n_blocks, bw, _ = w.shape
    tiles = []
    for c, k0 in enumerate(starts):
        c0, c1 = c * GATE_TILE, (c + 1) * GATE_TILE
        tile = jnp.zeros((win, GATE_TILE), w.dtype)
        for n in range(c0 // bw, (c1 - 1) // bw + 1):
            g0, g1 = max(bw * n, c0), min(bw * (n + 1), c1)
            piece = w[n][:, g0 - bw * n : g1 - bw * n]
            r0 = bw * n - k0
            tile = tile + jnp.pad(piece, ((r0, win - r0 - bw), (g0 - c0, c1 - g1)))
        tiles.append(tile)
    return jnp.stack(tiles)


def _lru_body(rec_ref, gate_ref, cinit_ref, hinit_ref, cw_ref, cb_ref, wa_ref, wi_ref, ba_ref, bi_ref,
              lam_ref, y_ref, hlast_ref, prev_ref, h_ref, *, tm, n_valid, starts, win):
    @pl.when(pl.program_id(1) == 0)
    def _():
        prev_ref[...] = cinit_ref[0]
        h_ref[...] = hinit_ref[0]

    x = rec_ref[0]
    d_rnn = x.shape[1]
    prev = prev_ref[...]
    row8 = lax.broadcasted_iota(jnp.int32, (SUBLANES, 1), 0)
    row = lax.broadcasted_iota(jnp.int32, (tm, 1), 0)

    def delayed(k):
        r = pltpu.roll(x, k, axis=0)
        head = jnp.where(row8 < k, pltpu.roll(prev, k, axis=0), r[:SUBLANES])
        return head if tm == SUBLANES else jnp.concatenate([head, r[SUBLANES:]], axis=0)

    conv = cb_ref[...] + delayed(3) * cw_ref[0:1, :]
    conv = conv + delayed(2) * cw_ref[1:2, :]
    conv = conv + delayed(1) * cw_ref[2:3, :]
    conv = conv + x * cw_ref[3:4, :]
    prev_ref[...] = x[tm - SUBLANES :, :]
    conv_bf = conv.astype(BF16)
    sp = _softplus(-lam_ref[...])
    h_prev = h_ref[...]

    h_last = []
    for c, k0 in enumerate(starts):
        sl = slice(c * GATE_TILE, (c + 1) * GATE_TILE)
        window = conv_bf[:, k0 : k0 + win]
        r = jax.nn.sigmoid(jnp.dot(window, wa_ref[c], preferred_element_type=F32) + ba_ref[:, sl])
        ig = jax.nn.sigmoid(jnp.dot(window, wi_ref[c], preferred_element_type=F32) + bi_ref[:, sl])
        log_a = (-C_RG * r) * sp[:, sl]
        a = jnp.exp(log_a)
        xin = (jnp.sqrt(-(jnp.tanh(log_a) * (a * a + 1.0))) * ig) * conv[:, sl]
        if n_valid < tm:
            a = jnp.where(row < n_valid, a, 1.0)
            xin = jnp.where(row < n_valid, xin, 0.0)
        shift = 1
        while shift < tm:
            a_sh = jnp.where(row < shift, 1.0, pltpu.roll(a, shift, axis=0))
            x_sh = jnp.where(row < shift, 0.0, pltpu.roll(xin, shift, axis=0))
            xin = a * x_sh + xin
            a = a * a_sh
            shift *= 2
        h = a * h_prev[0:1, sl] + xin
        y_ref[0, :, sl] = (h * _gelu_tanh(gate_ref[0, :, sl])).astype(y_ref.dtype)
        h_last.append(h[tm - 1 :, :])
    h_new = jnp.broadcast_to(jnp.concatenate(h_last, axis=1), (SUBLANES, d_rnn))
    h_ref[...] = h_new
    hlast_ref[0] = h_new


def _rglru(u, conv_init, h_init, conv_w, conv_b, wa_pack, wi_pack, b_a, b_i, lam,
           starts, win, tm, n_valid):
    b, t, r = u.shape[0], u.shape[1], u.shape[2] // 2
    n_tiles = wa_pack.shape[0]

    def const2(bi, ti):
        return (0, 0)

    def const3(bi, ti):
        return (0, 0, 0)

    def per_seq(bi, ti):
        return (bi, 0, 0)

    def tile(bi, ti):
        return (bi, ti, 0)

    return pl.pallas_call(
        functools.partial(_lru_body, tm=tm, n_valid=n_valid, starts=tuple(starts), win=win),
        grid=(b, t // tm),
        in_specs=[
            pl.BlockSpec((1, tm, r), lambda bi, ti: (bi, ti, 1)),
            pl.BlockSpec((1, tm, r), tile),
            pl.BlockSpec((1, SUBLANES, r), per_seq),
            pl.BlockSpec((1, SUBLANES, r), per_seq),
            pl.BlockSpec((SUBLANES, r), const2),
            pl.BlockSpec((1, r), const2),
            pl.BlockSpec((n_tiles, win, GATE_TILE), const3),
            pl.BlockSpec((n_tiles, win, GATE_TILE), const3),
            pl.BlockSpec((1, r), const2),
            pl.BlockSpec((1, r), const2),
            pl.BlockSpec((1, r), const2),
        ],
        out_specs=[
            pl.BlockSpec((1, tm, r), tile),
            pl.BlockSpec((1, SUBLANES, r), per_seq),
        ],
        out_shape=[
            jax.ShapeDtypeStruct((b, t, r), BF16),
            jax.ShapeDtypeStruct((b, SUBLANES, r), F32),
        ],
        scratch_shapes=[pltpu.VMEM((SUBLANES, r), F32), pltpu.VMEM((SUBLANES, r), F32)],
        compiler_params=_params("parallel", "arbitrary"),
        name="rglru",
    )(u, u, conv_init, h_init, conv_w, conv_b, wa_pack, wi_pack, b_a, b_i, lam)


def _fox_project(xn, w_qkv, w_f, b_f, layer, d):
    q, = _matmul(xn, w_qkv, layer, 0, d, 1024, _epi_store, [BF16])
    k32, k16 = _matmul(xn, w_qkv, layer, d, d, 1024, _epi_store_both, [F32, BF16])
    v32, v16 = _matmul(xn, w_qkv, layer, 2 * d, d, 1024, _epi_store_both, [F32, BF16])
    lf, = _matmul(xn, w_f, layer, 0, LANES, LANES, _epi_log_sigmoid, [F32], bias=b_f)
    return q, k32, k16, v32, v16, lf


def kernel(x_prompt, x_sample, cache_k, cache_v, cache_logf, state_conv, state_h, page_table,
           norm_mix_pre, norm_mix_post, norm_ffn_pre, norm_ffn_post,
           fox_w_qkv, fox_w_f, fox_b_f, fox_w_o,
           lru_w_in, lru_conv_w, lru_conv_b, lru_w_a, lru_b_a, lru_w_i, lru_b_i, lru_lam, lru_w_out,
           ffn_w_gate, ffn_w_up, ffn_w_down):
    batch, seq, d = x_prompt.shape
    dec_batch, dec_seq, _ = x_sample.shape
    depth = norm_mix_pre.shape[0]
    n_heads = d // HEAD_DIM
    n_fox, n_pool, page = cache_logf.shape[:3]
    d_rnn = lru_lam.shape[1]
    block_w = lru_w_a.shape[2]
    m_p = batch * seq
    m_s = dec_batch * SAMPLE_ROWS
    pad_rows = SAMPLE_ROWS - dec_seq

    xp = x_prompt.reshape(m_p, d)
    xs = jnp.pad(x_sample, ((0, 0), (0, pad_rows), (0, 0))).reshape(m_s, d)

    k_rows = cache_k.reshape(n_fox * n_pool * page * n_heads, HEAD_DIM)
    v_rows = cache_v.reshape(n_fox * n_pool * page * n_heads, HEAD_DIM)
    lf_pages = cache_logf.reshape(n_fox * n_pool, 1, page * n_heads)
    starts, win = _gate_windows(d_rnn, block_w)

    w_qkv = fox_w_qkv.astype(BF16)
    w_f = jnp.pad(fox_w_f, ((0, 0), (0, 0), (0, LANES - n_heads))).astype(BF16)
    w_o = fox_w_o.astype(BF16)
    w_in = lru_w_in.astype(BF16)
    w_out = lru_w_out.astype(BF16)
    w_gate = ffn_w_gate.astype(BF16)
    w_up = ffn_w_up.astype(BF16)
    w_down = ffn_w_down.astype(BF16)

    kp_l, vp_l, lp_l, ks_l, vs_l, ls_l = [], [], [], [], [], []
    cp_l, hp_l, cs_l, hs_l = [], [], [], []
    xn_p = _rms_norm(xp, norm_mix_pre[0])
    xn_s = _rms_norm(xs, norm_mix_pre[0])
    for i in range(depth):
        j = i // 2
        g_post, g_ffn = norm_mix_post[i], norm_ffn_pre[i]
        g_next = norm_mix_pre[min(i + 1, depth - 1)]
        if i % 2 == 0:
            b_f = jnp.pad(fox_b_f[j], (0, LANES - n_heads)).reshape(1, LANES)

            q, k32, k16, v32, v16, lf = _fox_project(xn_p, w_qkv, w_f, b_f, j, d)
            lf = lf[:, :n_heads].reshape(batch, seq, n_heads)
            c = _cumsum_lanes(lf.transpose(0, 2, 1).reshape(batch * n_heads, seq))
            o = _fox_prompt_attention(q.reshape(batch, seq, d), k16.reshape(batch, seq, d),
                                      v16.reshape(batch, seq, d), c.reshape(batch, n_heads, seq))
            xp, xn_p = _matmul_post_norm(o.reshape(m_p, d), w_o, j, g_post, g_ffn, xp)
            kp_l.append(k32.reshape(batch, seq, n_heads, HEAD_DIM))
            vp_l.append(v32.reshape(batch, seq, n_heads, HEAD_DIM))
            lp_l.append(lf)

            q, k32, k16, v32, v16, lf = _fox_project(xn_s, w_qkv, w_f, b_f, j, d)
            lf = lf[:, :n_heads].reshape(dec_batch, SAMPLE_ROWS, n_heads)[:, :dec_seq]
            lf_new = jnp.pad(lf, ((0, 0), (0, pad_rows), (0, 0))).reshape(dec_batch, 1, SAMPLE_ROWS * n_heads)
            q_rows = q.reshape(dec_batch, SAMPLE_ROWS, n_heads, HEAD_DIM)[:, :dec_seq]
            o = _fox_sample_attention(
                page_table, q_rows.reshape(dec_batch, dec_seq * n_heads, HEAD_DIM), k_rows, v_rows, lf_pages,
                k16.reshape(dec_batch, SAMPLE_ROWS * n_heads, HEAD_DIM),
                v16.reshape(dec_batch, SAMPLE_ROWS * n_heads, HEAD_DIM),
                lf_new, j * n_pool, n_heads)
            o = jnp.pad(o.reshape(dec_batch, dec_seq, d), ((0, 0), (0, pad_rows), (0, 0)))
            xs, xn_s = _matmul_post_norm(o.reshape(m_s, d).astype(BF16), w_o, j, g_post, g_ffn, xs)
            ks_l.append(k32.reshape(dec_batch, SAMPLE_ROWS, n_heads, HEAD_DIM)[:, :dec_seq])
            vs_l.append(v32.reshape(dec_batch, SAMPLE_ROWS, n_heads, HEAD_DIM)[:, :dec_seq])
            ls_l.append(lf)
        else:
            conv_w = jnp.pad(lru_conv_w[j], ((0, SUBLANES - CONV_W), (0, 0)))
            conv_b = lru_conv_b[j].reshape(1, d_rnn)
            wa_pack = _pack_block_diag(lru_w_a[j], starts, win).astype(BF16)
            wi_pack = _pack_block_diag(lru_w_i[j], starts, win).astype(BF16)
            b_a = lru_b_a[j].reshape(1, d_rnn)
            b_i = lru_b_i[j].reshape(1, d_rnn)
            lam = lru_lam[j].reshape(1, d_rnn)

            def mixer(xn, n_seq, t, conv_init, h_init, tm, n_valid):
                u, = _matmul(xn, w_in, j, 0, 2 * d_rnn, d_rnn // 2, _epi_store, [F32])
                u = u.reshape(n_seq, t, 2 * d_rnn)
                y, h_last = _rglru(u, conv_init, h_init, conv_w, conv_b,
                                   wa_pack, wi_pack, b_a, b_i, lam, starts, win, tm, n_valid)
                return y.reshape(n_seq * t, d_rnn), u[:, :, d_rnn:], h_last[:, 0]

            zeros = jnp.zeros((batch, SUBLANES, d_rnn), F32)
            y, rec, h_last = mixer(xn_p, batch, seq, zeros, zeros, 256, 256)
            xp, xn_p = _matmul_post_norm(y, w_out, j, g_post, g_ffn, xp)
            cp_l.append(rec[:, seq - (CONV_W - 1) :])
            hp_l.append(h_last)

            conv_init = jnp.pad(state_conv[j], ((0, 0), (SUBLANES - (CONV_W - 1), 0), (0, 0)))
            h_init = jnp.pad(state_h[j][:, None, :], ((0, 0), (0, SUBLANES - 1), (0, 0)))
            y, rec, h_last = mixer(xn_s, dec_batch, SAMPLE_ROWS, conv_init, h_init, SAMPLE_ROWS, dec_seq)
            xs, xn_s = _matmul_post_norm(y, w_out, j, g_post, g_ffn, xs)
            full = jnp.concatenate([state_conv[j], rec[:, :dec_seq]], axis=1)
            cs_l.append(full[:, dec_seq:])
            hs_l.append(h_last)

        xp, xn_p = _ffn(xn_p, xp, w_gate, w_up, w_down, i, norm_ffn_post[i], g_next)
        xs, xn_s = _ffn(xn_s, xs, w_gate, w_up, w_down, i, norm_ffn_post[i], g_next)

    y_sample = xs.reshape(dec_batch, SAMPLE_ROWS, d)[:, :dec_seq]
    return (xp.reshape(batch, seq, d), y_sample,
            jnp.stack(kp_l), jnp.stack(vp_l), jnp.stack(lp_l),
            jnp.stack(ks_l), jnp.stack(vs_l), jnp.stack(ls_l),
            jnp.stack(cp_l), jnp.stack(hp_l), jnp.stack(cs_l), jnp.stack(hs_l))
```

```python
import functools
import math

import jax
import jax.numpy as jnp
from jax import lax
from jax.experimental import pallas as pl
from jax.experimental.pallas import tpu as pltpu

F32 = jnp.float32
BF16 = jnp.bfloat16

RMS_EPS = 1e-6
NEG_INF = -1e30
LOG2_E = math.log2(math.e)
HEAD_DIM = 128
QK_SCALE = LOG2_E / math.sqrt(HEAD_DIM)
C_RG = 8.0
CONV_W = 4

LANES = 128
SUBLANES = 8
MXU_COLS = 256
VMEM_LIMIT_BYTES = 56 * 1024 * 1024

SAMPLE_ROWS = SUBLANES
GATE_TILE = MXU_COLS


def _params(*semantics):
    return pltpu.CompilerParams(dimension_semantics=semantics, vmem_limit_bytes=VMEM_LIMIT_BYTES)


def _nt_dot(a, b):
    return lax.dot_general(a, b, (((1,), (1,)), ((), ())), preferred_element_type=F32)


def _exact_dot(x, u):
    hi = x.astype(BF16)
    r1 = x - hi.astype(F32)
    mid = r1.astype(BF16)
    lo = (r1 - mid.astype(F32)).astype(BF16)
    dot = functools.partial(jnp.dot, preferred_element_type=F32)
    return dot(hi, u) + dot(mid, u) + dot(lo, u)


def _tri(n, strict_lower):
    rp = lax.broadcasted_iota(jnp.int32, (n, n), 0)
    r = lax.broadcasted_iota(jnp.int32, (n, n), 1)
    return jnp.where(rp > r if strict_lower else rp <= r, 1.0, 0.0).astype(BF16)


def _log_sigmoid(z):
    return -(jnp.maximum(-z, 0.0) + jnp.log1p(jnp.exp(-jnp.abs(z))))


def _softplus(z):
    return jnp.maximum(z, 0.0) + jnp.log1p(jnp.exp(-jnp.abs(z)))


def _gelu_tanh(x):
    cdf = 0.5 * (1.0 + jnp.tanh(math.sqrt(2.0 / math.pi) * (x + 0.044715 * (x * x * x))))
    return x * cdf


NORM_CHUNK = 512


def _rms_norm_body(x_ref, g_ref, o_ref):
    x = x_ref[...]
    var = jnp.mean(x * x, axis=-1, keepdims=True)
    o_ref[...] = ((x * lax.rsqrt(var + RMS_EPS)) * g_ref[...]).astype(o_ref.dtype)


def _rms_norm(x, g, tm=512):
    m, d = x.shape
    tm = min(tm, m)
    return pl.pallas_call(
        _rms_norm_body,
        grid=(m // tm,),
        in_specs=[pl.BlockSpec((tm, d), lambda i: (i, 0)), pl.BlockSpec((1, d), lambda i: (0, 0))],
        out_specs=pl.BlockSpec((tm, d), lambda i: (i, 0)),
        out_shape=jax.ShapeDtypeStruct((m, d), BF16),
        compiler_params=_params("parallel"),
        name="rms_norm",
    )(x, g.reshape(1, d))


def _residual_norm_epilogue(y_chunk, res_ref, g_ref, gn_ref, o_ref, xn_ref, d):
    n_c = d // NORM_CHUNK
    chunks = [slice(c * NORM_CHUNK, (c + 1) * NORM_CHUNK) for c in range(n_c)]

    def sum_sq(get):
        total = None
        for c in range(n_c):
            v = get(c)
            part = jnp.sum(v * v, axis=-1, keepdims=True)
            total = part if total is None else total + part
        return total

    inv = lax.rsqrt(sum_sq(y_chunk) / d + RMS_EPS)
    for c, sl in enumerate(chunks):
        o_ref[:, sl] = res_ref[:, sl] + (y_chunk(c) * inv) * g_ref[:, sl]
    inv_n = lax.rsqrt(sum_sq(lambda c: o_ref[:, chunks[c]]) / d + RMS_EPS)
    for sl in chunks:
        xn_ref[:, sl] = ((o_ref[:, sl] * inv_n) * gn_ref[:, sl]).astype(xn_ref.dtype)


def _epi_store(acc, o_ref):
    o_ref[...] = acc.astype(o_ref.dtype)


def _epi_store_query(acc, o_ref):
    o_ref[...] = (acc * QK_SCALE).astype(o_ref.dtype)


def _epi_store_both(acc, o32_ref, o16_ref):
    o32_ref[...] = acc
    o16_ref[...] = acc.astype(BF16)


def _epi_log_sigmoid(acc, b_ref, o_ref):
    o_ref[...] = _log_sigmoid(acc + b_ref[...])


def _mm_body(x_ref, w_ref, *rest, epilogue):
    epilogue(jnp.dot(x_ref[...], w_ref[...], preferred_element_type=F32), *rest)


def _matmul(x, w, layer, col_off, n_cols, tn, epilogue, out_dtypes, bias=None, tm=1024):
    m, k = x.shape
    tm = min(tm, m)
    assert col_off % tn == 0
    off = col_off // tn
    in_specs = [
        pl.BlockSpec((tm, k), lambda i, j: (i, 0)),
        pl.BlockSpec((None, k, tn), lambda i, j: (layer, 0, j + off)),
    ]
    args = [x, w]
    if bias is not None:
        in_specs.append(pl.BlockSpec((1, tn), lambda i, j: (0, j)))
        args.append(bias)
    return pl.pallas_call(
        functools.partial(_mm_body, epilogue=epilogue),
        grid=(m // tm, n_cols // tn),
        in_specs=in_specs,
        out_specs=[pl.BlockSpec((tm, tn), lambda i, j: (i, j)) for _ in out_dtypes],
        out_shape=[jax.ShapeDtypeStruct((m, n_cols), dt) for dt in out_dtypes],
        compiler_params=_params("parallel", "arbitrary"),
        name="mm_" + epilogue.__name__.removeprefix("_epi_"),
    )(*args)


def _mm_post_body(a_ref, w_ref, g_ref, gn_ref, res_ref, o_ref, xn_ref, y_ref, *, n_j, d):
    j = pl.program_id(1)
    y_ref[j] = jnp.dot(a_ref[...], w_ref[...], preferred_element_type=F32)

    @pl.when(j == n_j - 1)
    def _():
        _residual_norm_epilogue(lambda c: y_ref[c], res_ref, g_ref, gn_ref, o_ref, xn_ref, d)


def _matmul_post_norm(a, w, layer, g, g_next, res, tm=512):
    m, k = a.shape
    d = w.shape[2]
    tm = min(tm, m)
    n_j = d // NORM_CHUNK
    row = pl.BlockSpec((1, d), lambda i, j: (0, 0))
    tile = pl.BlockSpec((tm, d), lambda i, j: (i, 0))
    return pl.pallas_call(
        functools.partial(_mm_post_body, n_j=n_j, d=d),
        grid=(m // tm, n_j),
        in_specs=[
            pl.BlockSpec((tm, k), lambda i, j: (i, 0)),
            pl.BlockSpec((None, k, NORM_CHUNK), lambda i, j: (layer, 0, j)),
            row, row, tile,
        ],
        out_specs=[tile, tile],
        out_shape=[jax.ShapeDtypeStruct((m, d), F32), jax.ShapeDtypeStruct((m, d), BF16)],
        scratch_shapes=[pltpu.VMEM((n_j, tm, NORM_CHUNK), F32)],
        compiler_params=_params("parallel", "arbitrary"),
        name="mm_post_norm",
    )(a, w, g.reshape(1, d), g_next.reshape(1, d), res)


def _ffn_body(xn_ref, x_ref, wg_ref, wu_ref, wd_ref, g_ref, gn_ref, o_ref, xn_out_ref, acc_ref, *, n_f, d):
    f = pl.program_id(1)

    @pl.when(f == 0)
    def _():
        acc_ref[...] = jnp.zeros(acc_ref.shape, F32)

    xn = xn_ref[...]
    gate = jnp.dot(xn, wg_ref[...], preferred_element_type=F32)
    up = jnp.dot(xn, wu_ref[...], preferred_element_type=F32)
    h = ((gate * jax.nn.sigmoid(gate)) * up).astype(BF16)
    for c in range(d // NORM_CHUNK):
        sl = slice(c * NORM_CHUNK, (c + 1) * NORM_CHUNK)
        acc_ref[:, sl] += jnp.dot(h, wd_ref[:, sl], preferred_element_type=F32)

    @pl.when(f == n_f - 1)
    def _():
        _residual_norm_epilogue(lambda c: acc_ref[:, c * NORM_CHUNK : (c + 1) * NORM_CHUNK],
                                x_ref, g_ref, gn_ref, o_ref, xn_out_ref, d)


def _ffn(xn, x, w_gate, w_up, w_down, layer, g, g_next, tm=512, tf=512):
    m, d = x.shape
    d_ff = w_gate.shape[2]
    tm = min(tm, m)
    n_f = d_ff // tf
    row = pl.BlockSpec((1, d), lambda i, f: (0, 0))
    tile = pl.BlockSpec((tm, d), lambda i, f: (i, 0))
    return pl.pallas_call(
        functools.partial(_ffn_body, n_f=n_f, d=d),
        grid=(m // tm, n_f),
        in_specs=[
            tile, tile,
            pl.BlockSpec((None, d, tf), lambda i, f: (layer, 0, f)),
            pl.BlockSpec((None, d, tf), lambda i, f: (layer, 0, f)),
            pl.BlockSpec((None, tf, d), lambda i, f: (layer, f, 0)),
            row, row,
        ],
        out_specs=[tile, tile],
        out_shape=[jax.ShapeDtypeStruct((m, d), F32), jax.ShapeDtypeStruct((m, d), BF16)],
        scratch_shapes=[pltpu.VMEM((tm, d), F32)],
        compiler_params=_params("parallel", "arbitrary"),
        name="ffn",
    )(xn, x, w_gate, w_up, w_down, g.reshape(1, d), g_next.reshape(1, d))


def _cumsum_body(x_ref, o_ref):
    rows, n = x_ref.shape
    u = _tri(LANES, strict_lower=False)
    carry = jnp.zeros((rows, 1), F32)
    for c in range(n // LANES):
        sl = slice(c * LANES, (c + 1) * LANES)
        cs = _exact_dot(x_ref[:, sl], u) + carry
        o_ref[:, sl] = cs
        carry = cs[:, LANES - 1 :]


def _cumsum_lanes(x):
    return pl.pallas_call(
        _cumsum_body,
        out_shape=jax.ShapeDtypeStruct(x.shape, F32),
        name="cumsum_lanes",
    )(x)


def _flash_body(q_ref, k_ref, v_ref, c_ref, o_ref, m_ref, l_ref, acc_ref, *, tq, n_hh):
    qi = pl.program_id(2)
    m_ref[...] = jnp.full(m_ref.shape, NEG_INF, F32)
    l_ref[...] = jnp.zeros(l_ref.shape, F32)
    acc_ref[...] = jnp.zeros(acc_ref.shape, F32)
    reps = tq // LANES

    def tile(j, masked):
        start = pl.multiple_of(j * tq, tq)
        for hh in range(n_hh):
            cols = slice(hh * HEAD_DIM, (hh + 1) * HEAD_DIM)
            k = k_ref[0, pl.ds(start, tq), cols]
            v = v_ref[0, pl.ds(start, tq), cols]
            s = _nt_dot(q_ref[0, :, cols], k) - c_ref[0, hh, pl.ds(j, 1), :] * LOG2_E
            if masked:
                row = lax.broadcasted_iota(jnp.int32, (tq, tq), 0)
                col = lax.broadcasted_iota(jnp.int32, (tq, tq), 1)
                s = jnp.where(col <= row, s, NEG_INF)
            m_prev = m_ref[hh]
            m_new = jnp.maximum(m_prev, jnp.max(s, axis=-1, keepdims=True))
            alpha = jnp.exp2(m_prev - m_new)
            p = jnp.exp2(s - jnp.tile(m_new, (1, reps))).astype(BF16)
            pv = jnp.dot(p, jnp.concatenate([v, jnp.ones_like(v)], axis=1), preferred_element_type=F32)
            acc_ref[hh] = alpha * acc_ref[hh] + pv[:, :HEAD_DIM]
            l_ref[hh] = alpha * l_ref[hh] + pv[:, HEAD_DIM:]
            m_ref[hh] = m_new

    def body(j, carry):
        tile(j, False)
        return carry

    lax.fori_loop(0, qi, body, 0)
    tile(qi, True)
    for hh in range(n_hh):
        cols = slice(hh * HEAD_DIM, (hh + 1) * HEAD_DIM)
        o_ref[0, :, cols] = (acc_ref[hh] / l_ref[hh]).astype(o_ref.dtype)


def _fox_prompt_attention(q, k, v, c, tq=512, n_hh=4):
    b, s, d = q.shape
    h = d // HEAD_DIM
    n_q = s // tq
    c4 = c.reshape(b, h, n_q, tq)
    w = n_hh * HEAD_DIM
    return pl.pallas_call(
        functools.partial(_flash_body, tq=tq, n_hh=n_hh),
        grid=(b, h // n_hh, n_q),
        in_specs=[
            pl.BlockSpec((1, tq, w), lambda bi, hi, qi: (bi, qi, hi)),
            pl.BlockSpec((1, s, w), lambda bi, hi, qi: (bi, 0, hi)),
            pl.BlockSpec((1, s, w), lambda bi, hi, qi: (bi, 0, hi)),
            pl.BlockSpec((1, n_hh, n_q, tq), lambda bi, hi, qi: (bi, hi, 0, 0)),
        ],
        out_specs=pl.BlockSpec((1, tq, w), lambda bi, hi, qi: (bi, qi, hi)),
        out_shape=jax.ShapeDtypeStruct((b, s, d), BF16),
        scratch_shapes=[
            pltpu.VMEM((n_hh, tq, LANES), F32),
            pltpu.VMEM((n_hh, tq, LANES), F32),
            pltpu.VMEM((n_hh, tq, HEAD_DIM), F32),
        ],
        compiler_params=_params("parallel", "parallel", "arbitrary"),
        name="fox_prompt_attention",
    )(q, k, v, c4)


def _lane_scan(x, lane, step, limit=LANES, reverse=False, cyclic=False):
    shift = step
    while shift < limit:
        if cyclic:
            x = x + pltpu.roll(x, shift, axis=1)
        elif reverse:
            x = x + jnp.where(lane + shift < limit, pltpu.roll(x, LANES - shift, axis=1), 0.0)
        else:
            x = x + jnp.where(lane >= shift, pltpu.roll(x, shift, axis=1), 0.0)
        shift *= 2
    return x


def _sublane_suffix(x, row, cyclic=False):
    shift = 1
    while shift < SUBLANES:
        rolled = pltpu.roll(x, SUBLANES - shift, axis=0)
        x = x + (rolled if cyclic else jnp.where(row + shift < SUBLANES, rolled, 0.0))
        shift *= 2
    return x


def _decode_body(pt_ref, q_ref, *refs, n_pp, n_groups, n_tok, n_hh):
    del pt_ref
    k_refs, v_refs, lf_refs = refs[:n_pp], refs[n_pp : 2 * n_pp], refs[2 * n_pp : 3 * n_pp]
    kn_ref, vn_ref, lfn_ref, o_ref, bias_ref, m_ref, l_ref, acc_ref, carry_ref = refs[3 * n_pp :]
    p = pl.program_id(1)
    n_halves = q_ref.shape[1]
    n_rows = n_tok * n_hh
    lane = lax.broadcasted_iota(jnp.int32, (SUBLANES, LANES), 1)
    sub = lax.broadcasted_iota(jnp.int32, (SUBLANES, LANES), 0)
    row_q = lax.broadcasted_iota(jnp.int32, (n_rows, LANES), 0)
    lane_q = lax.broadcasted_iota(jnp.int32, (n_rows, LANES), 1)
    same_head = row_q % n_hh == lane_q % n_hh
    head_mask = jnp.where(same_head, 0.0, NEG_INF)

    @pl.when(p == 0)
    def _():
        bias_ref[...] = jnp.full(bias_ref.shape, NEG_INF, F32)
        m_ref[...] = jnp.full(m_ref.shape, 0.5 * NEG_INF, F32)
        l_ref[...] = jnp.zeros(l_ref.shape, F32)
        acc_ref[...] = jnp.zeros(acc_ref.shape, F32)
        carry_ref[...] = jnp.zeros(carry_ref.shape, F32)

    def attend(half, kbs, vbs, biases):
        q = q_ref[0, half]
        ss = []
        for kb, bias in zip(kbs, biases):
            s = _nt_dot(q, kb)
            ss.append(jnp.concatenate(
                [s[:, a * LANES : (a + 1) * LANES] + bias_a for a, bias_a in enumerate(bias)], axis=1))
        m_prev = m_ref[half]
        m_new = m_prev
        for s in ss:
            m_new = jnp.maximum(m_new, jnp.max(s, axis=-1, keepdims=True))
        alpha = jnp.exp2(m_prev - m_new)
        l_new = alpha * l_ref[half]
        acc = alpha * acc_ref[half]
        for s, vb in zip(ss, vbs):
            pr = jnp.exp2(s - jnp.tile(m_new, (1, s.shape[1] // LANES)))
            l_new = l_new + jnp.sum(pr, axis=-1, keepdims=True)
            acc = acc + jnp.dot(pr.astype(BF16), vb, preferred_element_type=F32)
        l_ref[half] = l_new
        acc_ref[half] = acc
        m_ref[half] = m_new

    def half_rows(ref, half):
        x = ref[:, half]
        return x.reshape(x.shape[0] * x.shape[1], x.shape[2]).astype(BF16)

    for half in range(n_halves):
        attend(half,
               [half_rows(k_ref, half) for k_ref in k_refs],
               [half_rows(v_ref, half) for v_ref in v_refs],
               [[head_mask + bias_ref[i, half, a : a + 1, :] for a in range(SUBLANES)] for i in range(n_pp)])

    for half in range(n_halves):
        carry = carry_ref[half]
        for i, lf_ref in enumerate(lf_refs):
            lf = lf_ref[0, half] * LOG2_E
            later = jnp.where(lane + n_hh < LANES, pltpu.roll(lf, LANES - n_hh, axis=1), 0.0)
            in_row = _lane_scan(later, lane, n_hh, reverse=True)
            row_total = _lane_scan(lf, lane, n_hh, cyclic=True)
            below = jnp.where(sub + 1 < SUBLANES, pltpu.roll(row_total, SUBLANES - 1, axis=0), 0.0)
            bias_ref[i, half] = in_row + _sublane_suffix(below, sub) + carry
            carry = carry + _sublane_suffix(row_total, sub, cyclic=True)
        carry_ref[half] = carry

    @pl.when(p == n_groups)
    def _():
        for half in range(n_halves):
            c_new = _lane_scan(jnp.broadcast_to(lfn_ref[0, half], (SUBLANES, LANES)) * LOG2_E, lane, n_hh,
                               limit=n_rows)
            causal = same_head & (lane_q // n_hh <= row_q // n_hh)
            attend(half, [kn_ref[0, half]], [vn_ref[0, half]], [[jnp.where(causal, -c_new[0:1, :], NEG_INF)]])
            o_ref[0, half] = acc_ref[half] / l_ref[half]


def _fox_sample_attention(page_table, q, k_cache, v_cache, lf_tiles, k_new, v_new, lf_new, base, n_pp=4):
    b, n_halves, n_rows, hd = q.shape
    n_hh = k_cache.shape[2]
    n_pages = page_table.shape[1]
    page = k_cache.shape[0] // lf_tiles.shape[0]
    assert n_hh == SUBLANES and page * n_hh == SUBLANES * LANES and n_pages % n_pp == 0
    n_groups = n_pages // n_pp

    def page_of(bi, group, pt, slot):
        return base + pt[bi, n_pages - 1 - (group * n_pp + slot)]

    def page_kv(bi, pi, pt, *, slot):
        return (page_of(bi, jnp.maximum(pi - 1, 0), pt, slot), 0, 0, 0)

    def page_lf(bi, pi, pt, *, slot):
        return (page_of(bi, jnp.minimum(pi, n_groups - 1), pt, slot), 0, 0, 0)

    def per_seq(bi, pi, pt):
        return (bi, 0, 0, 0)

    slots = range(n_pp)
    kv_specs = [pl.BlockSpec((page, n_halves, n_hh, hd), functools.partial(page_kv, slot=i)) for i in slots]
    lf_specs = [pl.BlockSpec((1, n_halves, SUBLANES, LANES), functools.partial(page_lf, slot=i))
                for i in slots]
    state = pltpu.VMEM((n_halves, n_rows, LANES), F32)
    grid_spec = pltpu.PrefetchScalarGridSpec(
        num_scalar_prefetch=1,
        grid=(b, n_groups + 1),
        in_specs=[pl.BlockSpec((1, n_halves, n_rows, hd), per_seq), *kv_specs, *kv_specs, *lf_specs,
                  pl.BlockSpec((1, n_halves, LANES, hd), per_seq),
                  pl.BlockSpec((1, n_halves, LANES, hd), per_seq),
                  pl.BlockSpec((1, n_halves, 1, LANES), per_seq)],
        out_specs=pl.BlockSpec((1, n_halves, n_rows, hd), per_seq),
        scratch_shapes=[
            pltpu.VMEM((n_pp, n_halves, SUBLANES, LANES), F32),
            state, state, state,
            pltpu.VMEM((n_halves, SUBLANES, LANES), F32),
        ],
    )
    return pl.pallas_call(
        functools.partial(_decode_body, n_pp=n_pp, n_groups=n_groups, n_tok=n_rows // n_hh, n_hh=n_hh),
        grid_spec=grid_spec,
        out_shape=jax.ShapeDtypeStruct((b, n_halves, n_rows, hd), F32),
        compiler_params=_params("parallel", "arbitrary"),
        name="fox_sample_attention",
    )(page_table, q, *[k_cache] * n_pp, *[v_cache] * n_pp, *[lf_tiles] * n_pp, k_new, v_new, lf_new)


def _gate_windows(d_rnn, block_w):
    spans = []
    for c0 in range(0, d_rnn, GATE_TILE):
        n_lo, n_hi = c0 // block_w, (c0 + GATE_TILE - 1) // block_w
        spans.append((block_w * n_lo // LANES * LANES, block_w * (n_hi + 1)))
    win = max(-(-(hi - lo) // LANES) * LANES for lo, hi in spans)
    starts = [min(lo, d_rnn - win) for lo, _ in spans]
    assert all(s + win >= hi for s, (_, hi) in zip(starts, spans))
    return starts, win


def _pack_block_diag(w, starts, win):
    n_blocks, bw, _ = w.shape
    tiles = []
    for c, k0 in enumerate(starts):
        c0, c1 = c * GATE_TILE, (c + 1) * GATE_TILE
        tile = jnp.zeros((win, GATE_TILE), w.dtype)
        for n in range(c0 // bw, (c1 - 1) // bw + 1):
            g0, g1 = max(bw * n, c0), min(bw * (n + 1), c1)
            piece = w[n][:, g0 - bw * n : g1 - bw * n]
            r0 = bw * n - k0
            tile = tile + jnp.pad(piece, ((r0, win - r0 - bw), (g0 - c0, c1 - g1)))
        tiles.append(tile)
    return jnp.stack(tiles)


def _lru_body(rec_ref, gate_ref, cinit_ref, hinit_ref, cw_ref, cb_ref, wa_ref, wi_ref, ba_ref, bi_ref,
              lam_ref, y_ref, hlast_ref, prev_ref, h_ref, *, tm, n_valid, starts, win):
    @pl.when(pl.program_id(1) == 0)
    def _():
        prev_ref[...] = cinit_ref[0]
        h_ref[...] = hinit_ref[0]

    x = rec_ref[0]
    d_rnn = x.shape[1]
    prev = prev_ref[...]
    row8 = lax.broadcasted_iota(jnp.int32, (SUBLANES, 1), 0)
    row = lax.broadcasted_iota(jnp.int32, (tm, 1), 0)
    row_in_group = lax.broadcasted_iota(jnp.int32, (1, SUBLANES, 1), 1)

    def delayed(k):
        r = pltpu.roll(x, k, axis=0)
        head = jnp.where(row8 < k, pltpu.roll(prev, k, axis=0), r[:SUBLANES])
        return head if tm == SUBLANES else jnp.concatenate([head, r[SUBLANES:]], axis=0)

    conv = cb_ref[...] + delayed(3) * cw_ref[0:1, :]
    conv = conv + delayed(2) * cw_ref[1:2, :]
    conv = conv + delayed(1) * cw_ref[2:3, :]
    conv = conv + x * cw_ref[3:4, :]
    prev_ref[...] = x[tm - SUBLANES :, :]
    conv_bf = conv.astype(BF16)
    sp = _softplus(-lam_ref[...])
    h_prev = h_ref[...]

    h_last = []
    for c, k0 in enumerate(starts):
        sl = slice(c * GATE_TILE, (c + 1) * GATE_TILE)
        window = conv_bf[:, k0 : k0 + win]
        r = jax.nn.sigmoid(jnp.dot(window, wa_ref[c], preferred_element_type=F32) + ba_ref[:, sl])
        ig = jax.nn.sigmoid(jnp.dot(window, wi_ref[c], preferred_element_type=F32) + bi_ref[:, sl])
        log_a = (-C_RG * r) * sp[:, sl]
        a = jnp.exp(log_a)
        xin = (jnp.sqrt(-(jnp.tanh(log_a) * (a * a + 1.0))) * ig) * conv[:, sl]
        if n_valid < tm:
            a = jnp.where(row < n_valid, a, 1.0)
            xin = jnp.where(row < n_valid, xin, 0.0)
        a = a.reshape(tm // SUBLANES, SUBLANES, GATE_TILE)
        xin = xin.reshape(tm // SUBLANES, SUBLANES, GATE_TILE)
        shift = 1
        while shift < SUBLANES:
            a_sh = jnp.where(row_in_group < shift, 1.0, pltpu.roll(a, shift, axis=1))
            x_sh = jnp.where(row_in_group < shift, 0.0, pltpu.roll(xin, shift, axis=1))
            xin = a * x_sh + xin
            a = a * a_sh
            shift *= 2
        h_in = h_prev[0:1, sl]
        groups = []
        for g in range(tm // SUBLANES):
            h_g = a[g] * h_in + xin[g]
            groups.append(h_g)
            h_in = h_g[SUBLANES - 1 :, :]
        h = groups[0] if len(groups) == 1 else jnp.concatenate(groups, axis=0)
        y_ref[0, :, sl] = (h * _gelu_tanh(gate_ref[0, :, sl])).astype(y_ref.dtype)
        h_last.append(h_in)
    h_new = jnp.broadcast_to(jnp.concatenate(h_last, axis=1), (SUBLANES, d_rnn))
    h_ref[...] = h_new
    hlast_ref[0] = h_new


def _rglru(u, conv_init, h_init, conv_w, conv_b, wa_pack, wi_pack, b_a, b_i, lam,
           starts, win, tm, n_valid):
    b, t, r = u.shape[0], u.shape[1], u.shape[2] // 2
    n_tiles = wa_pack.shape[0]

    def const2(bi, ti):
        return (0, 0)

    def const3(bi, ti):
        return (0, 0, 0)

    def per_seq(bi, ti):
        return (bi, 0, 0)

    def tile(bi, ti):
        return (bi, ti, 0)

    return pl.pallas_call(
        functools.partial(_lru_body, tm=tm, n_valid=n_valid, starts=tuple(starts), win=win),
        grid=(b, t // tm),
        in_specs=[
            pl.BlockSpec((1, tm, r), lambda bi, ti: (bi, ti, 1)),
            pl.BlockSpec((1, tm, r), tile),
            pl.BlockSpec((1, SUBLANES, r), per_seq),
            pl.BlockSpec((1, SUBLANES, r), per_seq),
            pl.BlockSpec((SUBLANES, r), const2),
            pl.BlockSpec((1, r), const2),
            pl.BlockSpec((n_tiles, win, GATE_TILE), const3),
            pl.BlockSpec((n_tiles, win, GATE_TILE), const3),
            pl.BlockSpec((1, r), const2),
            pl.BlockSpec((1, r), const2),
            pl.BlockSpec((1, r), const2),
        ],
        out_specs=[
            pl.BlockSpec((1, tm, r), tile),
            pl.BlockSpec((1, SUBLANES, r), per_seq),
        ],
        out_shape=[
            jax.ShapeDtypeStruct((b, t, r), BF16),
            jax.ShapeDtypeStruct((b, SUBLANES, r), F32),
        ],
        scratch_shapes=[pltpu.VMEM((SUBLANES, r), F32), pltpu.VMEM((SUBLANES, r), F32)],
        compiler_params=_params("parallel", "arbitrary"),
        name="rglru",
    )(u, u, conv_init, h_init, conv_w, conv_b, wa_pack, wi_pack, b_a, b_i, lam)


def _fox_project(xn, w_qkv, w_f, b_f, layer, d):
    q, = _matmul(xn, w_qkv, layer, 0, d, 1024, _epi_store_query, [BF16])
    k32, k16 = _matmul(xn, w_qkv, layer, d, d, 1024, _epi_store_both, [F32, BF16])
    v32, v16 = _matmul(xn, w_qkv, layer, 2 * d, d, 1024, _epi_store_both, [F32, BF16])
    lf, = _matmul(xn, w_f, layer, 0, LANES, LANES, _epi_log_sigmoid, [F32], bias=b_f)
    return q, k32, k16, v32, v16, lf


def kernel(x_prompt, x_sample, cache_k, cache_v, cache_logf, state_conv, state_h, page_table,
           norm_mix_pre, norm_mix_post, norm_ffn_pre, norm_ffn_post,
           fox_w_qkv, fox_w_f, fox_b_f, fox_w_o,
           lru_w_in, lru_conv_w, lru_conv_b, lru_w_a, lru_b_a, lru_w_i, lru_b_i, lru_lam, lru_w_out,
           ffn_w_gate, ffn_w_up, ffn_w_down):
    batch, seq, d = x_prompt.shape
    dec_batch, dec_seq, _ = x_sample.shape
    depth = norm_mix_pre.shape[0]
    n_heads = d // HEAD_DIM
    n_fox, n_pool, page = cache_logf.shape[:3]
    d_rnn = lru_lam.shape[1]
    block_w = lru_w_a.shape[2]
    m_p = batch * seq
    m_s = dec_batch * SAMPLE_ROWS
    pad_rows = SAMPLE_ROWS - dec_seq

    xp = x_prompt.reshape(m_p, d)
    xs = jnp.pad(x_sample, ((0, 0), (0, pad_rows), (0, 0))).reshape(m_s, d)

    halves = n_heads // SUBLANES
    k_cache = cache_k.reshape(n_fox * n_pool * page, halves, SUBLANES, HEAD_DIM)
    v_cache = cache_v.reshape(n_fox * n_pool * page, halves, SUBLANES, HEAD_DIM)
    lf_tiles = cache_logf.reshape(n_fox * n_pool, page, halves, SUBLANES).transpose(0, 2, 1, 3)
    lf_tiles = lf_tiles.reshape(n_fox * n_pool, halves, SUBLANES, LANES)
    starts, win = _gate_windows(d_rnn, block_w)

    w_qkv = fox_w_qkv.astype(BF16)
    w_f = jnp.pad(fox_w_f, ((0, 0), (0, 0), (0, LANES - n_heads))).astype(BF16)
    w_o = fox_w_o.astype(BF16)
    w_in = lru_w_in.astype(BF16)
    w_out = lru_w_out.astype(BF16)
    w_gate = ffn_w_gate.astype(BF16)
    w_up = ffn_w_up.astype(BF16)
    w_down = ffn_w_down.astype(BF16)

    kp_l, vp_l, lp_l, ks_l, vs_l, ls_l = [], [], [], [], [], []
    cp_l, hp_l, cs_l, hs_l = [], [], [], []
    xn_p = _rms_norm(xp, norm_mix_pre[0])
    xn_s = _rms_norm(xs, norm_mix_pre[0])
    for i in range(depth):
        j = i // 2
        g_post, g_ffn = norm_mix_post[i], norm_ffn_pre[i]
        g_next = norm_mix_pre[min(i + 1, depth - 1)]
        if i % 2 == 0:
            b_f = jnp.pad(fox_b_f[j], (0, LANES - n_heads)).reshape(1, LANES)

            q, k32, k16, v32, v16, lf = _fox_project(xn_p, w_qkv, w_f, b_f, j, d)
            lf = lf[:, :n_heads].reshape(batch, seq, n_heads)
            c = _cumsum_lanes(lf.transpose(0, 2, 1).reshape(batch * n_heads, seq))
            o = _fox_prompt_attention(q.reshape(batch, seq, d), k16.reshape(batch, seq, d),
                                      v16.reshape(batch, seq, d), c.reshape(batch, n_heads, seq))
            xp, xn_p = _matmul_post_norm(o.reshape(m_p, d), w_o, j, g_post, g_ffn, xp)
            kp_l.append(k32.reshape(batch, seq, n_heads, HEAD_DIM))
            vp_l.append(v32.reshape(batch, seq, n_heads, HEAD_DIM))
            lp_l.append(lf)

            q, k32, k16, v32, v16, lf = _fox_project(xn_s, w_qkv, w_f, b_f, j, d)
            lf = lf[:, :n_heads].reshape(dec_batch, SAMPLE_ROWS, n_heads)[:, :dec_seq]
            def split_heads(x, rows):
                x = x.reshape(dec_batch, SAMPLE_ROWS, halves, SUBLANES, HEAD_DIM)[:, :rows]
                return x.transpose(0, 2, 1, 3, 4).reshape(dec_batch, halves, rows * SUBLANES, HEAD_DIM)

            pad_new = ((0, 0), (0, 0), (0, LANES - SAMPLE_ROWS * SUBLANES), (0, 0))
            lf_new = jnp.pad(lf, ((0, 0), (0, pad_rows), (0, 0)))
            lf_new = lf_new.reshape(dec_batch, SAMPLE_ROWS, halves, SUBLANES).transpose(0, 2, 1, 3)
            lf_new = jnp.pad(lf_new.reshape(dec_batch, halves, 1, SAMPLE_ROWS * SUBLANES),
                             ((0, 0), (0, 0), (0, 0), (0, LANES - SAMPLE_ROWS * SUBLANES)))
            o = _fox_sample_attention(
                page_table, split_heads(q, dec_seq), k_cache, v_cache, lf_tiles,
                jnp.pad(split_heads(k16, SAMPLE_ROWS), pad_new),
                jnp.pad(split_heads(v16, SAMPLE_ROWS), pad_new),
                lf_new, j * n_pool)
            o = o.reshape(dec_batch, halves, dec_seq, SUBLANES, HEAD_DIM).transpose(0, 2, 1, 3, 4)
            o = jnp.pad(o.reshape(dec_batch, dec_seq, d), ((0, 0), (0, pad_rows), (0, 0)))
            xs, xn_s = _matmul_post_norm(o.reshape(m_s, d).astype(BF16), w_o, j, g_post, g_ffn, xs)
            ks_l.append(k32.reshape(dec_batch, SAMPLE_ROWS, n_heads, HEAD_DIM)[:, :dec_seq])
            vs_l.append(v32.reshape(dec_batch, SAMPLE_ROWS, n_heads, HEAD_DIM)[:, :dec_seq])
            ls_l.append(lf)
        else:
            conv_w = jnp.pad(lru_conv_w[j], ((0, SUBLANES - CONV_W), (0, 0)))
            conv_b = lru_conv_b[j].reshape(1, d_rnn)
            wa_pack = _pack_block_diag(lru_w_a[j], starts, win).astype(BF16)
            wi_pack = _pack_block_diag(lru_w_i[j], starts, win).astype(BF16)
            b_a = lru_b_a[j].reshape(1, d_rnn)
            b_i = lru_b_i[j].reshape(1, d_rnn)
            lam = lru_lam[j].reshape(1, d_rnn)

            def mixer(xn, n_seq, t, conv_init, h_init, tm, n_valid):
                u, = _matmul(xn, w_in, j, 0, 2 * d_rnn, d_rnn // 2, _epi_store, [F32])
                u = u.reshape(n_seq, t, 2 * d_rnn)
                y, h_last = _rglru(u, conv_init, h_init, conv_w, conv_b,
                                   wa_pack, wi_pack, b_a, b_i, lam, starts, win, tm, n_valid)
                return y.reshape(n_seq * t, d_rnn), u[:, :, d_rnn:], h_last[:, 0]

            zeros = jnp.zeros((batch, SUBLANES, d_rnn), F32)
            y, rec, h_last = mixer(xn_p, batch, seq, zeros, zeros, 256, 256)
            xp, xn_p = _matmul_post_norm(y, w_out, j, g_post, g_ffn, xp)
            cp_l.append(rec[:, seq - (CONV_W - 1) :])
            hp_l.append(h_last)

            conv_init = jnp.pad(state_conv[j], ((0, 0), (SUBLANES - (CONV_W - 1), 0), (0, 0)))
            h_init = jnp.pad(state_h[j][:, None, :], ((0, 0), (0, SUBLANES - 1), (0, 0)))
            y, rec, h_last = mixer(xn_s, dec_batch, SAMPLE_ROWS, conv_init, h_init, SAMPLE_ROWS, dec_seq)
            xs, xn_s = _matmul_post_norm(y, w_out, j, g_post, g_ffn, xs)
            full = jnp.concatenate([state_conv[j], rec[:, :dec_seq]], axis=1)
            cs_l.append(full[:, dec_seq:])
            hs_l.append(h_last)

        xp, xn_p = _ffn(xn_p, xp, w_gate, w_up, w_down, i, norm_ffn_post[i], g_next)
        xs, xn_s = _ffn(xn_s, xs, w_gate, w_up, w_down, i, norm_ffn_post[i], g_next)

    y_sample = xs.reshape(dec_batch, SAMPLE_ROWS, d)[:, :dec_seq]
    return (xp.reshape(batch, seq, d), y_sample,
            jnp.stack(kp_l), jnp.stack(vp_l), jnp.stack(lp_l),
            jnp.stack(ks_l), jnp.stack(vs_l), jnp.stack(ls_l),
            jnp.stack(cp_l), jnp.stack(hp_l), jnp.stack(cs_l), jnp.stack(hs_l))
```

```python
import functools
import math

import jax
import jax.numpy as jnp
from jax import lax
from jax.experimental import pallas as pl
from jax.experimental.pallas import tpu as pltpu

F32 = jnp.float32
BF16 = jnp.bfloat16

RMS_EPS = 1e-6
NEG_INF = -1e30
LOG2_E = math.log2(math.e)
HEAD_DIM = 128
QK_SCALE = LOG2_E / math.sqrt(HEAD_DIM)
C_RG = 8.0
CONV_W = 4

LANES = 128
SUBLANES = 8
MXU_COLS = 256
VMEM_LIMIT_BYTES = 56 * 1024 * 1024

SAMPLE_ROWS = SUBLANES
GATE_TILE = MXU_COLS


def _params(*semantics):
    return pltpu.CompilerParams(dimension_semantics=semantics, vmem_limit_bytes=VMEM_LIMIT_BYTES)


def _nt_dot(a, b):
    return lax.dot_general(a, b, (((1,), (1,)), ((), ())), preferred_element_type=F32)


def _exact_dot(x, u):
    hi = x.astype(BF16)
    r1 = x - hi.astype(F32)
    mid = r1.astype(BF16)
    lo = (r1 - mid.astype(F32)).astype(BF16)
    dot = functools.partial(jnp.dot, preferred_element_type=F32)
    return dot(hi, u) + dot(mid, u) + dot(lo, u)


def _tri(n, strict_lower):
    rp = lax.broadcasted_iota(jnp.int32, (n, n), 0)
    r = lax.broadcasted_iota(jnp.int32, (n, n), 1)
    return jnp.where(rp > r if strict_lower else rp <= r, 1.0, 0.0).astype(BF16)


def _log_sigmoid(z):
    return -(jnp.maximum(-z, 0.0) + jnp.log1p(jnp.exp(-jnp.abs(z))))


def _softplus(z):
    return jnp.maximum(z, 0.0) + jnp.log1p(jnp.exp(-jnp.abs(z)))


def _gelu_tanh(x):
    cdf = 0.5 * (1.0 + jnp.tanh(math.sqrt(2.0 / math.pi) * (x + 0.044715 * (x * x * x))))
    return x * cdf


NORM_CHUNK = 512


def _rms_norm_body(x_ref, g_ref, o_ref):
    x = x_ref[...]
    var = jnp.mean(x * x, axis=-1, keepdims=True)
    o_ref[...] = ((x * lax.rsqrt(var + RMS_EPS)) * g_ref[...]).astype(o_ref.dtype)


def _rms_norm(x, g, tm=512):
    m, d = x.shape
    tm = min(tm, m)
    return pl.pallas_call(
        _rms_norm_body,
        grid=(m // tm,),
        in_specs=[pl.BlockSpec((tm, d), lambda i: (i, 0)), pl.BlockSpec((1, d), lambda i: (0, 0))],
        out_specs=pl.BlockSpec((tm, d), lambda i: (i, 0)),
        out_shape=jax.ShapeDtypeStruct((m, d), BF16),
        compiler_params=_params("parallel"),
        name="rms_norm",
    )(x, g.reshape(1, d))


def _residual_norm_epilogue(y_chunk, res_ref, g_ref, gn_ref, o_ref, xn_ref, d):
    n_c = d // NORM_CHUNK
    chunks = [slice(c * NORM_CHUNK, (c + 1) * NORM_CHUNK) for c in range(n_c)]

    def sum_sq(get):
        total = None
        for c in range(n_c):
            v = get(c)
            part = jnp.sum(v * v, axis=-1, keepdims=True)
            total = part if total is None else total + part
        return total

    inv = lax.rsqrt(sum_sq(y_chunk) / d + RMS_EPS)
    for c, sl in enumerate(chunks):
        o_ref[:, sl] = res_ref[:, sl] + (y_chunk(c) * inv) * g_ref[:, sl]
    inv_n = lax.rsqrt(sum_sq(lambda c: o_ref[:, chunks[c]]) / d + RMS_EPS)
    for sl in chunks:
        xn_ref[:, sl] = ((o_ref[:, sl] * inv_n) * gn_ref[:, sl]).astype(xn_ref.dtype)


def _epi_store(acc, o_ref):
    o_ref[...] = acc.astype(o_ref.dtype)


def _epi_store_query(acc, o_ref):
    o_ref[...] = (acc * QK_SCALE).astype(o_ref.dtype)


def _epi_log_sigmoid(acc, b_ref, o_ref):
    o_ref[...] = _log_sigmoid(acc + b_ref[...])


def _mm_body(x_ref, w_ref, *rest, epilogue):
    epilogue(jnp.dot(x_ref[...], w_ref[...], preferred_element_type=F32), *rest)


def _matmul(x, w, layer, col_off, n_cols, tn, epilogue, out_dtypes, bias=None, tm=1024):
    m, k = x.shape
    tm = min(tm, m)
    assert col_off % tn == 0
    off = col_off // tn
    in_specs = [
        pl.BlockSpec((tm, k), lambda i, j: (i, 0)),
        pl.BlockSpec((None, k, tn), lambda i, j: (layer, 0, j + off)),
    ]
    args = [x, w]
    if bias is not None:
        in_specs.append(pl.BlockSpec((1, tn), lambda i, j: (0, j)))
        args.append(bias)
    return pl.pallas_call(
        functools.partial(_mm_body, epilogue=epilogue),
        grid=(m // tm, n_cols // tn),
        in_specs=in_specs,
        out_specs=[pl.BlockSpec((tm, tn), lambda i, j: (i, j)) for _ in out_dtypes],
        out_shape=[jax.ShapeDtypeStruct((m, n_cols), dt) for dt in out_dtypes],
        compiler_params=_params("parallel", "arbitrary"),
        name="mm_" + epilogue.__name__.removeprefix("_epi_"),
    )(*args)


def _kv_body(x_ref, w_ref, *refs):
    o32_ref, o16_ref = refs[-2:]
    acc = jnp.dot(x_ref[...], w_ref[...], preferred_element_type=F32)
    o32_ref[...] = acc.reshape(o32_ref.shape)
    o16_ref[...] = acc.astype(BF16)


def _kv_projection(x, w, layer, col_off, stacked, n_layers, tm=1024, tn=1024):
    m, k = x.shape
    d = w.shape[2] // 3
    tm = min(tm, m)
    n_heads, hpt = d // HEAD_DIM, tn // HEAD_DIM
    off, row0 = col_off // tn, layer * (m // tm)
    in_specs = [
        pl.BlockSpec((tm, k), lambda i, j: (i, 0)),
        pl.BlockSpec((None, k, tn), lambda i, j: (layer, 0, j + off)),
    ]
    args = [x, w]
    if stacked is not None:
        in_specs.append(pl.BlockSpec(memory_space=pl.ANY))
        args.append(stacked)
    return pl.pallas_call(
        _kv_body,
        grid=(m // tm, d // tn),
        in_specs=in_specs,
        out_specs=[pl.BlockSpec((tm, hpt, HEAD_DIM), lambda i, j: (row0 + i, j, 0)),
                   pl.BlockSpec((tm, tn), lambda i, j: (i, j))],
        out_shape=[jax.ShapeDtypeStruct((n_layers * m, n_heads, HEAD_DIM), F32),
                   jax.ShapeDtypeStruct((m, d), BF16)],
        input_output_aliases={} if stacked is None else {2: 0},
        compiler_params=_params("parallel", "arbitrary"),
        name="kv_projection",
    )(*args)


def _mm_post_body(a_ref, w_ref, g_ref, gn_ref, res_ref, o_ref, xn_ref, y_ref, *, n_j, d):
    j = pl.program_id(1)
    y_ref[j] = jnp.dot(a_ref[...], w_ref[...], preferred_element_type=F32)
    per_j = y_ref.shape[2] // NORM_CHUNK

    def y_chunk(c):
        lo = (c % per_j) * NORM_CHUNK
        return y_ref[c // per_j, :, lo : lo + NORM_CHUNK]

    @pl.when(j == n_j - 1)
    def _():
        _residual_norm_epilogue(y_chunk, res_ref, g_ref, gn_ref, o_ref, xn_ref, d)


def _matmul_post_norm(a, w, layer, g, g_next, res, tm=512, tn=1024):
    m, k = a.shape
    d = w.shape[2]
    tm = min(tm, m)
    n_j = d // tn
    row = pl.BlockSpec((1, d), lambda i, j: (0, 0))
    tile = pl.BlockSpec((tm, d), lambda i, j: (i, 0))
    return pl.pallas_call(
        functools.partial(_mm_post_body, n_j=n_j, d=d),
        grid=(m // tm, n_j),
        in_specs=[
            pl.BlockSpec((tm, k), lambda i, j: (i, 0)),
            pl.BlockSpec((None, k, tn), lambda i, j: (layer, 0, j)),
            row, row, tile,
        ],
        out_specs=[tile, tile],
        out_shape=[jax.ShapeDtypeStruct((m, d), F32), jax.ShapeDtypeStruct((m, d), BF16)],
        scratch_shapes=[pltpu.VMEM((n_j, tm, tn), F32)],
        compiler_params=_params("parallel", "arbitrary"),
        name="mm_post_norm",
    )(a, w, g.reshape(1, d), g_next.reshape(1, d), res)


def _ffn_body(xn_ref, x_ref, wg_ref, wu_ref, wd_ref, g_ref, gn_ref, o_ref, xn_out_ref, acc_ref, *, n_f, d):
    f = pl.program_id(1)

    @pl.when(f == 0)
    def _():
        acc_ref[...] = jnp.zeros(acc_ref.shape, F32)

    xn = xn_ref[...]
    gate = jnp.dot(xn, wg_ref[...], preferred_element_type=F32)
    up = jnp.dot(xn, wu_ref[...], preferred_element_type=F32)
    h = ((gate * jax.nn.sigmoid(gate)) * up).astype(BF16)
    for c in range(d // NORM_CHUNK):
        sl = slice(c * NORM_CHUNK, (c + 1) * NORM_CHUNK)
        acc_ref[:, sl] += jnp.dot(h, wd_ref[:, sl], preferred_element_type=F32)

    @pl.when(f == n_f - 1)
    def _():
        _residual_norm_epilogue(lambda c: acc_ref[:, c * NORM_CHUNK : (c + 1) * NORM_CHUNK],
                                x_ref, g_ref, gn_ref, o_ref, xn_out_ref, d)


def _ffn(xn, x, w_gate, w_up, w_down, layer, g, g_next, tm=512, tf=512):
    m, d = x.shape
    d_ff = w_gate.shape[2]
    tm = min(tm, m)
    n_f = d_ff // tf
    row = pl.BlockSpec((1, d), lambda i, f: (0, 0))
    tile = pl.BlockSpec((tm, d), lambda i, f: (i, 0))
    return pl.pallas_call(
        functools.partial(_ffn_body, n_f=n_f, d=d),
        grid=(m // tm, n_f),
        in_specs=[
            tile, tile,
            pl.BlockSpec((None, d, tf), lambda i, f: (layer, 0, f)),
            pl.BlockSpec((None, d, tf), lambda i, f: (layer, 0, f)),
            pl.BlockSpec((None, tf, d), lambda i, f: (layer, f, 0)),
            row, row,
        ],
        out_specs=[tile, tile],
        out_shape=[jax.ShapeDtypeStruct((m, d), F32), jax.ShapeDtypeStruct((m, d), BF16)],
        scratch_shapes=[pltpu.VMEM((tm, d), F32)],
        compiler_params=_params("parallel", "arbitrary"),
        name="ffn",
    )(xn, x, w_gate, w_up, w_down, g.reshape(1, d), g_next.reshape(1, d))


def _cumsum_body(x_ref, o_ref):
    rows, n = x_ref.shape
    u = _tri(LANES, strict_lower=False)
    carry = jnp.zeros((rows, 1), F32)
    for c in range(n // LANES):
        sl = slice(c * LANES, (c + 1) * LANES)
        cs = _exact_dot(x_ref[:, sl], u) + carry
        o_ref[:, sl] = cs
        carry = cs[:, LANES - 1 :]


def _cumsum_lanes(x):
    return pl.pallas_call(
        _cumsum_body,
        out_shape=jax.ShapeDtypeStruct(x.shape, F32),
        name="cumsum_lanes",
    )(x)


def _flash_body(q_ref, k_ref, v_ref, c_ref, o_ref, m_ref, l_ref, acc_ref, *, tq, n_hh):
    qi = pl.program_id(2)
    m_ref[...] = jnp.full(m_ref.shape, NEG_INF, F32)
    l_ref[...] = jnp.zeros(l_ref.shape, F32)
    acc_ref[...] = jnp.zeros(acc_ref.shape, F32)
    reps = tq // LANES

    def tile(j, masked):
        start = pl.multiple_of(j * tq, tq)
        for hh in range(n_hh):
            cols = slice(hh * HEAD_DIM, (hh + 1) * HEAD_DIM)
            k = k_ref[0, pl.ds(start, tq), cols]
            v = v_ref[0, pl.ds(start, tq), cols]
            s = _nt_dot(q_ref[0, :, cols], k) - c_ref[0, hh, pl.ds(j, 1), :] * LOG2_E
            if masked:
                row = lax.broadcasted_iota(jnp.int32, (tq, tq), 0)
                col = lax.broadcasted_iota(jnp.int32, (tq, tq), 1)
                s = jnp.where(col <= row, s, NEG_INF)
            m_prev = m_ref[hh]
            m_new = jnp.maximum(m_prev, jnp.max(s, axis=-1, keepdims=True))
            alpha = jnp.exp2(m_prev - m_new)
            p = jnp.exp2(s - jnp.tile(m_new, (1, reps))).astype(BF16)
            pv = jnp.dot(p, jnp.concatenate([v, jnp.ones_like(v)], axis=1), preferred_element_type=F32)
            acc_ref[hh] = alpha * acc_ref[hh] + pv[:, :HEAD_DIM]
            l_ref[hh] = alpha * l_ref[hh] + pv[:, HEAD_DIM:]
            m_ref[hh] = m_new

    def body(j, carry):
        tile(j, False)
        return carry

    lax.fori_loop(0, qi, body, 0)
    tile(qi, True)
    for hh in range(n_hh):
        cols = slice(hh * HEAD_DIM, (hh + 1) * HEAD_DIM)
        o_ref[0, :, cols] = (acc_ref[hh] / l_ref[hh]).astype(o_ref.dtype)


def _fox_prompt_attention(q, k, v, c, tq=512, n_hh=4):
    b, s, d = q.shape
    h = d // HEAD_DIM
    n_q = s // tq
    c4 = c.reshape(b, h, n_q, tq)
    w = n_hh * HEAD_DIM
    return pl.pallas_call(
        functools.partial(_flash_body, tq=tq, n_hh=n_hh),
        grid=(b, h // n_hh, n_q),
        in_specs=[
            pl.BlockSpec((1, tq, w), lambda bi, hi, qi: (bi, qi, hi)),
            pl.BlockSpec((1, s, w), lambda bi, hi, qi: (bi, 0, hi)),
            pl.BlockSpec((1, s, w), lambda bi, hi, qi: (bi, 0, hi)),
            pl.BlockSpec((1, n_hh, n_q, tq), lambda bi, hi, qi: (bi, hi, 0, 0)),
        ],
        out_specs=pl.BlockSpec((1, tq, w), lambda bi, hi, qi: (bi, qi, hi)),
        out_shape=jax.ShapeDtypeStruct((b, s, d), BF16),
        scratch_shapes=[
            pltpu.VMEM((n_hh, tq, LANES), F32),
            pltpu.VMEM((n_hh, tq, LANES), F32),
            pltpu.VMEM((n_hh, tq, HEAD_DIM), F32),
        ],
        compiler_params=_params("parallel", "parallel", "arbitrary"),
        name="fox_prompt_attention",
    )(q, k, v, c4)


def _lane_scan(x, lane, step, limit=LANES, reverse=False, cyclic=False):
    shift = step
    while shift < limit:
        if cyclic:
            x = x + pltpu.roll(x, shift, axis=1)
        elif reverse:
            x = x + jnp.where(lane + shift < limit, pltpu.roll(x, LANES - shift, axis=1), 0.0)
        else:
            x = x + jnp.where(lane >= shift, pltpu.roll(x, shift, axis=1), 0.0)
        shift *= 2
    return x


def _sublane_suffix(x, row, cyclic=False):
    shift = 1
    while shift < SUBLANES:
        rolled = pltpu.roll(x, SUBLANES - shift, axis=0)
        x = x + (rolled if cyclic else jnp.where(row + shift < SUBLANES, rolled, 0.0))
        shift *= 2
    return x


def _decode_body(pt_ref, q_ref, *refs, n_pp, n_groups, n_tok, n_hh):
    del pt_ref
    k_refs, v_refs, lf_refs = refs[:n_pp], refs[n_pp : 2 * n_pp], refs[2 * n_pp : 3 * n_pp]
    kn_ref, vn_ref, lfn_ref, o_ref, bias_ref, m_ref, l_ref, acc_ref, carry_ref = refs[3 * n_pp :]
    p = pl.program_id(1)
    n_halves = q_ref.shape[1]
    n_rows = n_tok * n_hh
    lane = lax.broadcasted_iota(jnp.int32, (SUBLANES, LANES), 1)
    sub = lax.broadcasted_iota(jnp.int32, (SUBLANES, LANES), 0)
    row_q = lax.broadcasted_iota(jnp.int32, (n_rows, LANES), 0)
    lane_q = lax.broadcasted_iota(jnp.int32, (n_rows, LANES), 1)
    same_head = row_q % n_hh == lane_q % n_hh
    head_mask = jnp.where(same_head, 0.0, NEG_INF)

    @pl.when(p == 0)
    def _():
        bias_ref[...] = jnp.full(bias_ref.shape, NEG_INF, F32)
        m_ref[...] = jnp.full(m_ref.shape, 0.5 * NEG_INF, F32)
        l_ref[...] = jnp.zeros(l_ref.shape, F32)
        acc_ref[...] = jnp.zeros(acc_ref.shape, F32)
        carry_ref[...] = jnp.zeros(carry_ref.shape, F32)

    def attend(half, kbs, vbs, biases):
        q = q_ref[0, half]
        ss = []
        for kb, bias in zip(kbs, biases):
            s = _nt_dot(q, kb)
            ss.append(jnp.concatenate(
                [s[:, a * LANES : (a + 1) * LANES] + bias_a for a, bias_a in enumerate(bias)], axis=1))
        m_prev = m_ref[half]
        m_new = m_prev
        for s in ss:
            m_new = jnp.maximum(m_new, jnp.max(s, axis=-1, keepdims=True))
        alpha = jnp.exp2(m_prev - m_new)
        l_new = alpha * l_ref[half]
        acc = alpha * acc_ref[half]
        for s, vb in zip(ss, vbs):
            pr = jnp.exp2(s - jnp.tile(m_new, (1, s.shape[1] // LANES)))
            l_new = l_new + jnp.sum(pr, axis=-1, keepdims=True)
            acc = acc + jnp.dot(pr.astype(BF16), vb, preferred_element_type=F32)
        l_ref[half] = l_new
        acc_ref[half] = acc
        m_ref[half] = m_new

    def half_rows(ref, half):
        x = ref[:, half]
        return x.reshape(x.shape[0] * x.shape[1], x.shape[2]).astype(BF16)

    for half in range(n_halves):
        attend(half,
               [half_rows(k_ref, half) for k_ref in k_refs],
               [half_rows(v_ref, half) for v_ref in v_refs],
               [[head_mask + bias_ref[i, half, a : a + 1, :] for a in range(SUBLANES)] for i in range(n_pp)])

    for half in range(n_halves):
        carry = carry_ref[half]
        for i, lf_ref in enumerate(lf_refs):
            lf = lf_ref[0, half] * LOG2_E
            later = jnp.where(lane + n_hh < LANES, pltpu.roll(lf, LANES - n_hh, axis=1), 0.0)
            in_row = _lane_scan(later, lane, n_hh, reverse=True)
            row_total = _lane_scan(lf, lane, n_hh, cyclic=True)
            below = jnp.where(sub + 1 < SUBLANES, pltpu.roll(row_total, SUBLANES - 1, axis=0), 0.0)
            bias_ref[i, half] = in_row + _sublane_suffix(below, sub) + carry
            carry = carry + _sublane_suffix(row_total, sub, cyclic=True)
        carry_ref[half] = carry

    @pl.when(p == n_groups)
    def _():
        for half in range(n_halves):
            c_new = _lane_scan(jnp.broadcast_to(lfn_ref[0, half], (SUBLANES, LANES)) * LOG2_E, lane, n_hh,
                               limit=n_rows)
            causal = same_head & (lane_q // n_hh <= row_q // n_hh)
            attend(half, [kn_ref[0, half]], [vn_ref[0, half]], [[jnp.where(causal, -c_new[0:1, :], NEG_INF)]])
            o_ref[0, half] = acc_ref[half] / l_ref[half]


def _fox_sample_attention(page_table, q, k_cache, v_cache, lf_tiles, k_new, v_new, lf_new, base, n_pp=8):
    b, n_halves, n_rows, hd = q.shape
    n_hh = k_cache.shape[2]
    n_pages = page_table.shape[1]
    page = k_cache.shape[0] // lf_tiles.shape[0]
    assert n_hh == SUBLANES and page * n_hh == SUBLANES * LANES and n_pages % n_pp == 0
    n_groups = n_pages // n_pp

    def page_of(bi, group, pt, slot):
        return base + pt[bi, n_pages - 1 - (group * n_pp + slot)]

    def page_kv(bi, pi, pt, *, slot):
        return (page_of(bi, jnp.maximum(pi - 1, 0), pt, slot), 0, 0, 0)

    def page_lf(bi, pi, pt, *, slot):
        return (page_of(bi, jnp.minimum(pi, n_groups - 1), pt, slot), 0, 0, 0)

    def per_seq(bi, pi, pt):
        return (bi, 0, 0, 0)

    slots = range(n_pp)
    kv_specs = [pl.BlockSpec((page, n_halves, n_hh, hd), functools.partial(page_kv, slot=i)) for i in slots]
    lf_specs = [pl.BlockSpec((1, n_halves, SUBLANES, LANES), functools.partial(page_lf, slot=i))
                for i in slots]
    state = pltpu.VMEM((n_halves, n_rows, LANES), F32)
    grid_spec = pltpu.PrefetchScalarGridSpec(
        num_scalar_prefetch=1,
        grid=(b, n_groups + 1),
        in_specs=[pl.BlockSpec((1, n_halves, n_rows, hd), per_seq), *kv_specs, *kv_specs, *lf_specs,
                  pl.BlockSpec((1, n_halves, LANES, hd), per_seq),
                  pl.BlockSpec((1, n_halves, LANES, hd), per_seq),
                  pl.BlockSpec((1, n_halves, 1, LANES), per_seq)],
        out_specs=pl.BlockSpec((1, n_halves, n_rows, hd), per_seq),
        scratch_shapes=[
            pltpu.VMEM((n_pp, n_halves, SUBLANES, LANES), F32),
            state, state, state,
            pltpu.VMEM((n_halves, SUBLANES, LANES), F32),
        ],
    )
    return pl.pallas_call(
        functools.partial(_decode_body, n_pp=n_pp, n_groups=n_groups, n_tok=n_rows // n_hh, n_hh=n_hh),
        grid_spec=grid_spec,
        out_shape=jax.ShapeDtypeStruct((b, n_halves, n_rows, hd), F32),
        compiler_params=_params("parallel", "arbitrary"),
        name="fox_sample_attention",
    )(page_table, q, *[k_cache] * n_pp, *[v_cache] * n_pp, *[lf_tiles] * n_pp, k_new, v_new, lf_new)


def _gate_windows(d_rnn, block_w):
    spans = []
    for c0 in range(0, d_rnn, GATE_TILE):
        n_lo, n_hi = c0 // block_w, (c0 + GATE_TILE - 1) // block_w
        spans.append((block_w * n_lo // LANES * LANES, block_w * (n_hi + 1)))
    win = max(-(-(hi - lo) // LANES) * LANES for lo, hi in spans)
    starts = [min(lo, d_rnn - win) for lo, _ in spans]
    assert all(s + win >= hi for s, (_, hi) in zip(starts, spans))
    return starts, win


def _pack_block_diag(w, starts, win):
    n_blocks, bw, _ = w.shape
    tiles = []
    for c, k0 in enumerate(starts):
        c0, c1 = c * GATE_TILE, (c + 1) * GATE_TILE
        tile = jnp.zeros((win, GATE_TILE), w.dtype)
        for n in range(c0 // bw, (c1 - 1) // bw + 1):
            g0, g1 = max(bw * n, c0), min(bw * (n + 1), c1)
            piece = w[n][:, g0 - bw * n : g1 - bw * n]
            r0 = bw * n - k0
            tile = tile + jnp.pad(piece, ((r0, win - r0 - bw), (g0 - c0, c1 - g1)))
        tiles.append(tile)
    return jnp.stack(tiles)


def _lru_body(rec_ref, gate_ref, cinit_ref, hinit_ref, cw_ref, cb_ref, wa_ref, wi_ref, ba_ref, bi_ref,
              lam_ref, y_ref, hlast_ref, prev_ref, h_ref, *, tm, n_valid, starts, win):
    @pl.when(pl.program_id(1) == 0)
    def _():
        prev_ref[...] = cinit_ref[0]
        h_ref[...] = hinit_ref[0]

    x = rec_ref[0]
    d_rnn = x.shape[1]
    prev = prev_ref[...]
    row8 = lax.broadcasted_iota(jnp.int32, (SUBLANES, 1), 0)
    row = lax.broadcasted_iota(jnp.int32, (tm, 1), 0)
    row_in_group = lax.broadcasted_iota(jnp.int32, (1, SUBLANES, 1), 1)

    def delayed(k):
        r = pltpu.roll(x, k, axis=0)
        head = jnp.where(row8 < k, pltpu.roll(prev, k, axis=0), r[:SUBLANES])
        return head if tm == SUBLANES else jnp.concatenate([head, r[SUBLANES:]], axis=0)

    conv = cb_ref[...] + delayed(3) * cw_ref[0:1, :]
    conv = conv + delayed(2) * cw_ref[1:2, :]
    conv = conv + delayed(1) * cw_ref[2:3, :]
    conv = conv + x * cw_ref[3:4, :]
    prev_ref[...] = x[tm - SUBLANES :, :]
    conv_bf = conv.astype(BF16)
    sp = _softplus(-lam_ref[...])
    h_prev = h_ref[...]

    h_last = []
    for c, k0 in enumerate(starts):
        sl = slice(c * GATE_TILE, (c + 1) * GATE_TILE)
        window = conv_bf[:, k0 : k0 + win]
        r = jax.nn.sigmoid(jnp.dot(window, wa_ref[c], preferred_element_type=F32) + ba_ref[:, sl])
        ig = jax.nn.sigmoid(jnp.dot(window, wi_ref[c], preferred_element_type=F32) + bi_ref[:, sl])
        log_a = (-C_RG * r) * sp[:, sl]
        a = jnp.exp(log_a)
        xin = (jnp.sqrt(-(jnp.tanh(log_a) * (a * a + 1.0))) * ig) * conv[:, sl]
        if n_valid < tm:
            a = jnp.where(row < n_valid, a, 1.0)
            xin = jnp.where(row < n_valid, xin, 0.0)
        a = a.reshape(tm // SUBLANES, SUBLANES, GATE_TILE)
        xin = xin.reshape(tm // SUBLANES, SUBLANES, GATE_TILE)
        shift = 1
        while shift < SUBLANES:
            a_sh = jnp.where(row_in_group < shift, 1.0, pltpu.roll(a, shift, axis=1))
            x_sh = jnp.where(row_in_group < shift, 0.0, pltpu.roll(xin, shift, axis=1))
            xin = a * x_sh + xin
            a = a * a_sh
            shift *= 2
        h_in = h_prev[0:1, sl]
        groups = []
        for g in range(tm // SUBLANES):
            h_g = a[g] * h_in + xin[g]
            groups.append(h_g)
            h_in = h_g[SUBLANES - 1 :, :]
        h = groups[0] if len(groups) == 1 else jnp.concatenate(groups, axis=0)
        y_ref[0, :, sl] = (h * _gelu_tanh(gate_ref[0, :, sl])).astype(y_ref.dtype)
        h_last.append(h_in)
    h_new = jnp.broadcast_to(jnp.concatenate(h_last, axis=1), (SUBLANES, d_rnn))
    h_ref[...] = h_new
    hlast_ref[0] = h_new


def _rglru(u, conv_init, h_init, conv_w, conv_b, wa_pack, wi_pack, b_a, b_i, lam,
           starts, win, tm, n_valid):
    b, t, r = u.shape[0], u.shape[1], u.shape[2] // 2
    n_tiles = wa_pack.shape[0]

    def const2(bi, ti):
        return (0, 0)

    def const3(bi, ti):
        return (0, 0, 0)

    def per_seq(bi, ti):
        return (bi, 0, 0)

    def tile(bi, ti):
        return (bi, ti, 0)

    return pl.pallas_call(
        functools.partial(_lru_body, tm=tm, n_valid=n_valid, starts=tuple(starts), win=win),
        grid=(b, t // tm),
        in_specs=[
            pl.BlockSpec((1, tm, r), lambda bi, ti: (bi, ti, 1)),
            pl.BlockSpec((1, tm, r), tile),
            pl.BlockSpec((1, SUBLANES, r), per_seq),
            pl.BlockSpec((1, SUBLANES, r), per_seq),
            pl.BlockSpec((SUBLANES, r), const2),
            pl.BlockSpec((1, r), const2),
            pl.BlockSpec((n_tiles, win, GATE_TILE), const3),
            pl.BlockSpec((n_tiles, win, GATE_TILE), const3),
            pl.BlockSpec((1, r), const2),
            pl.BlockSpec((1, r), const2),
            pl.BlockSpec((1, r), const2),
        ],
        out_specs=[
            pl.BlockSpec((1, tm, r), tile),
            pl.BlockSpec((1, SUBLANES, r), per_seq),
        ],
        out_shape=[
            jax.ShapeDtypeStruct((b, t, r), BF16),
            jax.ShapeDtypeStruct((b, SUBLANES, r), F32),
        ],
        scratch_shapes=[pltpu.VMEM((SUBLANES, r), F32), pltpu.VMEM((SUBLANES, r), F32)],
        compiler_params=_params("parallel", "arbitrary"),
        name="rglru",
    )(u, u, conv_init, h_init, conv_w, conv_b, wa_pack, wi_pack, b_a, b_i, lam)


def _fox_project(xn, w_qkv, w_f, b_f, layer, d, k_stack, v_stack):
    n_layers = w_qkv.shape[0]
    q, = _matmul(xn, w_qkv, layer, 0, d, 1024, _epi_store_query, [BF16])
    k_stack, k16 = _kv_projection(xn, w_qkv, layer, d, k_stack, n_layers)
    v_stack, v16 = _kv_projection(xn, w_qkv, layer, 2 * d, v_stack, n_layers)
    lf, = _matmul(xn, w_f, layer, 0, LANES, LANES, _epi_log_sigmoid, [F32], bias=b_f)
    return q, k16, v16, lf, k_stack, v_stack


def kernel(x_prompt, x_sample, cache_k, cache_v, cache_logf, state_conv, state_h, page_table,
           norm_mix_pre, norm_mix_post, norm_ffn_pre, norm_ffn_post,
           fox_w_qkv, fox_w_f, fox_b_f, fox_w_o,
           lru_w_in, lru_conv_w, lru_conv_b, lru_w_a, lru_b_a, lru_w_i, lru_b_i, lru_lam, lru_w_out,
           ffn_w_gate, ffn_w_up, ffn_w_down):
    batch, seq, d = x_prompt.shape
    dec_batch, dec_seq, _ = x_sample.shape
    depth = norm_mix_pre.shape[0]
    n_heads = d // HEAD_DIM
    n_fox, n_pool, page = cache_logf.shape[:3]
    d_rnn = lru_lam.shape[1]
    block_w = lru_w_a.shape[2]
    m_p = batch * seq
    m_s = dec_batch * SAMPLE_ROWS
    pad_rows = SAMPLE_ROWS - dec_seq

    xp = x_prompt.reshape(m_p, d)
    xs = jnp.pad(x_sample, ((0, 0), (0, pad_rows), (0, 0))).reshape(m_s, d)

    halves = n_heads // SUBLANES
    k_cache = cache_k.reshape(n_fox * n_pool * page, halves, SUBLANES, HEAD_DIM)
    v_cache = cache_v.reshape(n_fox * n_pool * page, halves, SUBLANES, HEAD_DIM)
    lf_tiles = cache_logf.reshape(n_fox * n_pool, page, halves, SUBLANES).transpose(0, 2, 1, 3)
    lf_tiles = lf_tiles.reshape(n_fox * n_pool, halves, SUBLANES, LANES)
    starts, win = _gate_windows(d_rnn, block_w)

    w_qkv = fox_w_qkv.astype(BF16)
    w_f = jnp.pad(fox_w_f, ((0, 0), (0, 0), (0, LANES - n_heads))).astype(BF16)
    w_o = fox_w_o.astype(BF16)
    w_in = lru_w_in.astype(BF16)
    w_out = lru_w_out.astype(BF16)
    w_gate = ffn_w_gate.astype(BF16)
    w_up = ffn_w_up.astype(BF16)
    w_down = ffn_w_down.astype(BF16)

    lp_l, ls_l = [], []
    cp_l, hp_l, cs_l, hs_l = [], [], [], []
    kp_stack = vp_stack = ks_stack = vs_stack = None
    xn_p = _rms_norm(xp, norm_mix_pre[0])
    xn_s = _rms_norm(xs, norm_mix_pre[0])
    for i in range(depth):
        j = i // 2
        g_post, g_ffn = norm_mix_post[i], norm_ffn_pre[i]
        g_next = norm_mix_pre[min(i + 1, depth - 1)]
        if i % 2 == 0:
            b_f = jnp.pad(fox_b_f[j], (0, LANES - n_heads)).reshape(1, LANES)

            q, k16, v16, lf, kp_stack, vp_stack = _fox_project(xn_p, w_qkv, w_f, b_f, j, d, kp_stack, vp_stack)
            lf = lf[:, :n_heads].reshape(batch, seq, n_heads)
            c = _cumsum_lanes(lf.transpose(0, 2, 1).reshape(batch * n_heads, seq))
            o = _fox_prompt_attention(q.reshape(batch, seq, d), k16.reshape(batch, seq, d),
                                      v16.reshape(batch, seq, d), c.reshape(batch, n_heads, seq))
            xp, xn_p = _matmul_post_norm(o.reshape(m_p, d), w_o, j, g_post, g_ffn, xp)
            lp_l.append(lf)

            q, k16, v16, lf, ks_stack, vs_stack = _fox_project(xn_s, w_qkv, w_f, b_f, j, d, ks_stack, vs_stack)
            lf = lf[:, :n_heads].reshape(dec_batch, SAMPLE_ROWS, n_heads)[:, :dec_seq]
            def split_heads(x, rows):
                x = x.reshape(dec_batch, SAMPLE_ROWS, halves, SUBLANES, HEAD_DIM)[:, :rows]
                return x.transpose(0, 2, 1, 3, 4).reshape(dec_batch, halves, rows * SUBLANES, HEAD_DIM)

            pad_new = ((0, 0), (0, 0), (0, LANES - SAMPLE_ROWS * SUBLANES), (0, 0))
            lf_new = jnp.pad(lf, ((0, 0), (0, pad_rows), (0, 0)))
            lf_new = lf_new.reshape(dec_batch, SAMPLE_ROWS, halves, SUBLANES).transpose(0, 2, 1, 3)
            lf_new = jnp.pad(lf_new.reshape(dec_batch, halves, 1, SAMPLE_ROWS * SUBLANES),
                             ((0, 0), (0, 0), (0, 0), (0, LANES - SAMPLE_ROWS * SUBLANES)))
            o = _fox_sample_attention(
                page_table, split_heads(q, dec_seq), k_cache, v_cache, lf_tiles,
                jnp.pad(split_heads(k16, SAMPLE_ROWS), pad_new),
                jnp.pad(split_heads(v16, SAMPLE_ROWS), pad_new),
                lf_new, j * n_pool)
            o = o.reshape(dec_batch, halves, dec_seq, SUBLANES, HEAD_DIM).transpose(0, 2, 1, 3, 4)
            o = jnp.pad(o.reshape(dec_batch, dec_seq, d), ((0, 0), (0, pad_rows), (0, 0)))
            xs, xn_s = _matmul_post_norm(o.reshape(m_s, d).astype(BF16), w_o, j, g_post, g_ffn, xs)
            ls_l.append(lf)
        else:
            conv_w = jnp.pad(lru_conv_w[j], ((0, SUBLANES - CONV_W), (0, 0)))
            conv_b = lru_conv_b[j].reshape(1, d_rnn)
            wa_pack = _pack_block_diag(lru_w_a[j], starts, win).astype(BF16)
            wi_pack = _pack_block_diag(lru_w_i[j], starts, win).astype(BF16)
            b_a = lru_b_a[j].reshape(1, d_rnn)
            b_i = lru_b_i[j].reshape(1, d_rnn)
            lam = lru_lam[j].reshape(1, d_rnn)

            def mixer(xn, n_seq, t, conv_init, h_init, tm, n_valid):
                u, = _matmul(xn, w_in, j, 0, 2 * d_rnn, d_rnn // 2, _epi_store, [F32])
                u = u.reshape(n_seq, t, 2 * d_rnn)
                y, h_last = _rglru(u, conv_init, h_init, conv_w, conv_b,
                                   wa_pack, wi_pack, b_a, b_i, lam, starts, win, tm, n_valid)
                return y.reshape(n_seq * t, d_rnn), u[:, :, d_rnn:], h_last[:, 0]

            zeros = jnp.zeros((batch, SUBLANES, d_rnn), F32)
            y, rec, h_last = mixer(xn_p, batch, seq, zeros, zeros, 256, 256)
            xp, xn_p = _matmul_post_norm(y, w_out, j, g_post, g_ffn, xp)
            cp_l.append(rec[:, seq - (CONV_W - 1) :])
            hp_l.append(h_last)

            conv_init = jnp.pad(state_conv[j], ((0, 0), (SUBLANES - (CONV_W - 1), 0), (0, 0)))
            h_init = jnp.pad(state_h[j][:, None, :], ((0, 0), (0, SUBLANES - 1), (0, 0)))
            y, rec, h_last = mixer(xn_s, dec_batch, SAMPLE_ROWS, conv_init, h_init, SAMPLE_ROWS, dec_seq)
            xs, xn_s = _matmul_post_norm(y, w_out, j, g_post, g_ffn, xs)
            full = jnp.concatenate([state_conv[j], rec[:, :dec_seq]], axis=1)
            cs_l.append(full[:, dec_seq:])
            hs_l.append(h_last)

        xp, xn_p = _ffn(xn_p, xp, w_gate, w_up, w_down, i, norm_ffn_post[i], g_next)
        xs, xn_s = _ffn(xn_s, xs, w_gate, w_up, w_down, i, norm_ffn_post[i], g_next)

    y_sample = xs.reshape(dec_batch, SAMPLE_ROWS, d)[:, :dec_seq]
    prompt_kv = (n_fox, batch, seq, n_heads, HEAD_DIM)
    sample_kv = (n_fox, dec_batch, SAMPLE_ROWS, n_heads, HEAD_DIM)
    return (xp.reshape(batch, seq, d), y_sample,
            kp_stack.reshape(prompt_kv), vp_stack.reshape(prompt_kv), jnp.stack(lp_l),
            ks_stack.reshape(sample_kv)[:, :, :dec_seq], vs_stack.reshape(sample_kv)[:, :, :dec_seq],
            jnp.stack(ls_l),
            jnp.stack(cp_l), jnp.stack(hp_l), jnp.stack(cs_l), jnp.stack(hs_l))
```

```python
import functools
import math

import jax
import jax.numpy as jnp
from jax import lax
from jax.experimental import pallas as pl
from jax.experimental.pallas import tpu as pltpu

F32 = jnp.float32
BF16 = jnp.bfloat16

RMS_EPS = 1e-6
NEG_INF = -1e30
LOG2_E = math.log2(math.e)
HEAD_DIM = 128
QK_SCALE = LOG2_E / math.sqrt(HEAD_DIM)
C_RG = 8.0
CONV_W = 4

LANES = 128
SUBLANES = 8
MXU_COLS = 256
VMEM_LIMIT_BYTES = 56 * 1024 * 1024

SAMPLE_ROWS = SUBLANES
GATE_TILE = MXU_COLS


def _params(*semantics):
    return pltpu.CompilerParams(dimension_semantics=semantics, vmem_limit_bytes=VMEM_LIMIT_BYTES)


def _nt_dot(a, b):
    return lax.dot_general(a, b, (((1,), (1,)), ((), ())), preferred_element_type=F32)


def _exact_dot(x, u):
    hi = x.astype(BF16)
    r1 = x - hi.astype(F32)
    mid = r1.astype(BF16)
    lo = (r1 - mid.astype(F32)).astype(BF16)
    dot = functools.partial(jnp.dot, preferred_element_type=F32)
    return dot(hi, u) + dot(mid, u) + dot(lo, u)


def _tri(n, strict_lower):
    rp = lax.broadcasted_iota(jnp.int32, (n, n), 0)
    r = lax.broadcasted_iota(jnp.int32, (n, n), 1)
    return jnp.where(rp > r if strict_lower else rp <= r, 1.0, 0.0).astype(BF16)


def _log_sigmoid(z):
    return -(jnp.maximum(-z, 0.0) + jnp.log1p(jnp.exp(-jnp.abs(z))))


def _softplus(z):
    return jnp.maximum(z, 0.0) + jnp.log1p(jnp.exp(-jnp.abs(z)))


def _gelu_tanh(x):
    cdf = 0.5 * (1.0 + jnp.tanh(math.sqrt(2.0 / math.pi) * (x + 0.044715 * (x * x * x))))
    return x * cdf


NORM_CHUNK = 512


def _rms_norm_body(x_ref, g_ref, o_ref):
    x = x_ref[...]
    var = jnp.mean(x * x, axis=-1, keepdims=True)
    o_ref[...] = ((x * lax.rsqrt(var + RMS_EPS)) * g_ref[...]).astype(o_ref.dtype)


def _rms_norm(x, g, tm=512):
    m, d = x.shape
    tm = min(tm, m)
    return pl.pallas_call(
        _rms_norm_body,
        grid=(m // tm,),
        in_specs=[pl.BlockSpec((tm, d), lambda i: (i, 0)), pl.BlockSpec((1, d), lambda i: (0, 0))],
        out_specs=pl.BlockSpec((tm, d), lambda i: (i, 0)),
        out_shape=jax.ShapeDtypeStruct((m, d), BF16),
        compiler_params=_params("parallel"),
        name="rms_norm",
    )(x, g.reshape(1, d))


def _residual_norm_epilogue(y_chunk, res_ref, g_ref, gn_ref, o_ref, xn_ref, d):
    n_c = d // NORM_CHUNK
    chunks = [slice(c * NORM_CHUNK, (c + 1) * NORM_CHUNK) for c in range(n_c)]

    def sum_sq(get):
        total = None
        for c in range(n_c):
            v = get(c)
            part = jnp.sum(v * v, axis=-1, keepdims=True)
            total = part if total is None else total + part
        return total

    inv = lax.rsqrt(sum_sq(y_chunk) / d + RMS_EPS)
    for c, sl in enumerate(chunks):
        o_ref[:, sl] = res_ref[:, sl] + (y_chunk(c) * inv) * g_ref[:, sl]
    inv_n = lax.rsqrt(sum_sq(lambda c: o_ref[:, chunks[c]]) / d + RMS_EPS)
    for sl in chunks:
        xn_ref[:, sl] = ((o_ref[:, sl] * inv_n) * gn_ref[:, sl]).astype(xn_ref.dtype)


def _epi_store(acc, o_ref):
    o_ref[...] = acc.astype(o_ref.dtype)


def _epi_store_query(acc, o_ref):
    o_ref[...] = (acc * QK_SCALE).astype(o_ref.dtype)


def _epi_log_sigmoid(acc, b_ref, o_ref):
    o_ref[...] = _log_sigmoid(acc + b_ref[...])


def _mm_body(x_ref, w_ref, *rest, epilogue):
    epilogue(jnp.dot(x_ref[...], w_ref[...], preferred_element_type=F32), *rest)


def _matmul(x, w, layer, col_off, n_cols, tn, epilogue, out_dtypes, bias=None, tm=1024):
    m, k = x.shape
    tm = min(tm, m)
    assert col_off % tn == 0
    off = col_off // tn
    in_specs = [
        pl.BlockSpec((tm, k), lambda i, j: (i, 0)),
        pl.BlockSpec((None, k, tn), lambda i, j: (layer, 0, j + off)),
    ]
    args = [x, w]
    if bias is not None:
        in_specs.append(pl.BlockSpec((1, tn), lambda i, j: (0, j)))
        args.append(bias)
    return pl.pallas_call(
        functools.partial(_mm_body, epilogue=epilogue),
        grid=(m // tm, n_cols // tn),
        in_specs=in_specs,
        out_specs=[pl.BlockSpec((tm, tn), lambda i, j: (i, j)) for _ in out_dtypes],
        out_shape=[jax.ShapeDtypeStruct((m, n_cols), dt) for dt in out_dtypes],
        compiler_params=_params("parallel", "arbitrary"),
        name="mm_" + epilogue.__name__.removeprefix("_epi_"),
    )(*args)


def _kv_body(x_ref, w_ref, *refs):
    o32_ref, o16_ref = refs[-2:]
    acc = jnp.dot(x_ref[...], w_ref[...], preferred_element_type=F32)
    o32_ref[...] = acc.reshape(o32_ref.shape)
    o16_ref[...] = acc.astype(BF16)


def _kv_projection(x, w, layer, col_off, stacked, n_layers, tm=1024, tn=1024):
    m, k = x.shape
    d = w.shape[2] // 3
    tm = min(tm, m)
    n_heads, hpt = d // HEAD_DIM, tn // HEAD_DIM
    off, row0 = col_off // tn, layer * (m // tm)
    in_specs = [
        pl.BlockSpec((tm, k), lambda i, j: (i, 0)),
        pl.BlockSpec((None, k, tn), lambda i, j: (layer, 0, j + off)),
    ]
    args = [x, w]
    if stacked is not None:
        in_specs.append(pl.BlockSpec(memory_space=pl.ANY))
        args.append(stacked)
    return pl.pallas_call(
        _kv_body,
        grid=(m // tm, d // tn),
        in_specs=in_specs,
        out_specs=[pl.BlockSpec((tm, hpt, HEAD_DIM), lambda i, j: (row0 + i, j, 0)),
                   pl.BlockSpec((tm, tn), lambda i, j: (i, j))],
        out_shape=[jax.ShapeDtypeStruct((n_layers * m, n_heads, HEAD_DIM), F32),
                   jax.ShapeDtypeStruct((m, d), BF16)],
        input_output_aliases={} if stacked is None else {2: 0},
        compiler_params=_params("parallel", "arbitrary"),
        name="kv_projection",
    )(*args)


def _mm_post_body(a_ref, w_ref, g_ref, gn_ref, res_ref, o_ref, xn_ref, y_ref, *, n_j, d):
    j = pl.program_id(1)
    y_ref[j] = jnp.dot(a_ref[...], w_ref[...], preferred_element_type=F32)
    per_j = y_ref.shape[2] // NORM_CHUNK

    def y_chunk(c):
        lo = (c % per_j) * NORM_CHUNK
        return y_ref[c // per_j, :, lo : lo + NORM_CHUNK]

    @pl.when(j == n_j - 1)
    def _():
        _residual_norm_epilogue(y_chunk, res_ref, g_ref, gn_ref, o_ref, xn_ref, d)


def _matmul_post_norm(a, w, layer, g, g_next, res, tm=512, tn=1024):
    m, k = a.shape
    d = w.shape[2]
    tm = min(tm, m)
    n_j = d // tn
    row = pl.BlockSpec((1, d), lambda i, j: (0, 0))
    tile = pl.BlockSpec((tm, d), lambda i, j: (i, 0))
    return pl.pallas_call(
        functools.partial(_mm_post_body, n_j=n_j, d=d),
        grid=(m // tm, n_j),
        in_specs=[
            pl.BlockSpec((tm, k), lambda i, j: (i, 0)),
            pl.BlockSpec((None, k, tn), lambda i, j: (layer, 0, j)),
            row, row, tile,
        ],
        out_specs=[tile, tile],
        out_shape=[jax.ShapeDtypeStruct((m, d), F32), jax.ShapeDtypeStruct((m, d), BF16)],
        scratch_shapes=[pltpu.VMEM((n_j, tm, tn), F32)],
        compiler_params=_params("parallel", "arbitrary"),
        name="mm_post_norm",
    )(a, w, g.reshape(1, d), g_next.reshape(1, d), res)


FFN_TILE = 512


def _ffn_step(f, n_f, xn_ref, x_ref, w_gate_up, w_down, g_ref, gn_ref, o_ref, xn_out_ref, acc_ref):
    d = acc_ref.shape[1]

    @pl.when(f == 0)
    def _():
        acc_ref[...] = jnp.zeros(acc_ref.shape, F32)

    gate_up = jnp.dot(xn_ref[...], w_gate_up[...], preferred_element_type=F32)
    gate, up = gate_up[:, :FFN_TILE], gate_up[:, FFN_TILE:]
    h = ((gate * jax.nn.sigmoid(gate)) * up).astype(BF16)
    for c in range(d // NORM_CHUNK):
        sl = slice(c * NORM_CHUNK, (c + 1) * NORM_CHUNK)
        acc_ref[:, sl] += jnp.dot(h, w_down[:, sl], preferred_element_type=F32)

    @pl.when(f == n_f - 1)
    def _():
        _residual_norm_epilogue(lambda c: acc_ref[:, c * NORM_CHUNK : (c + 1) * NORM_CHUNK],
                                x_ref, g_ref, gn_ref, o_ref, xn_out_ref, d)


def _ffn_body(xn_ref, x_ref, wgu_ref, wd_ref, g_ref, gn_ref, o_ref, xn_out_ref, acc_ref, *, n_f):
    _ffn_step(pl.program_id(1), n_f, xn_ref, x_ref, wgu_ref, wd_ref, g_ref, gn_ref, o_ref, xn_out_ref, acc_ref)


def _ffn(xn, x, w_gate_up, w_down, g, g_next, tm=512):
    m, d = x.shape
    n_f = w_gate_up.shape[0]
    row = pl.BlockSpec((1, d), lambda i, f: (0, 0))
    tile = pl.BlockSpec((tm, d), lambda i, f: (i, 0))
    return pl.pallas_call(
        functools.partial(_ffn_body, n_f=n_f),
        grid=(m // tm, n_f),
        in_specs=[
            tile, tile,
            pl.BlockSpec((None, d, 2 * FFN_TILE), lambda i, f: (f, 0, 0)),
            pl.BlockSpec((FFN_TILE, d), lambda i, f: (f, 0)),
            row, row,
        ],
        out_specs=[tile, tile],
        out_shape=[jax.ShapeDtypeStruct((m, d), F32), jax.ShapeDtypeStruct((m, d), BF16)],
        scratch_shapes=[pltpu.VMEM((tm, d), F32)],
        compiler_params=_params("parallel", "arbitrary"),
        name="ffn",
    )(xn, x, w_gate_up, w_down, g.reshape(1, d), g_next.reshape(1, d))


def _ffn_cast_body(xn_ref, x_ref, wg_ref, wu_ref, wd_ref, g_ref, gn_ref,
                   o_ref, xn_out_ref, wgu16_ref, wd16_ref, acc_ref, *, n_f):
    wgu16_ref[:, :FFN_TILE] = wg_ref[...].astype(BF16)
    wgu16_ref[:, FFN_TILE:] = wu_ref[...].astype(BF16)
    wd16_ref[...] = wd_ref[...].astype(BF16)
    _ffn_step(pl.program_id(0), n_f, xn_ref, x_ref, wgu16_ref, wd16_ref, g_ref, gn_ref, o_ref, xn_out_ref,
              acc_ref)


def _ffn_and_cast(xn, x, w_gate, w_up, w_down, layer, g, g_next):
    m, d = x.shape
    d_ff = w_gate.shape[2]
    n_f = d_ff // FFN_TILE
    row = pl.BlockSpec((1, d), lambda f: (0, 0))
    whole = pl.BlockSpec((m, d), lambda f: (0, 0))
    cols = pl.BlockSpec((None, d, FFN_TILE), lambda f: (layer, 0, f))
    return pl.pallas_call(
        functools.partial(_ffn_cast_body, n_f=n_f),
        grid=(n_f,),
        in_specs=[whole, whole, cols, cols,
                  pl.BlockSpec((None, FFN_TILE, d), lambda f: (layer, f, 0)), row, row],
        out_specs=[whole, whole,
                   pl.BlockSpec((None, d, 2 * FFN_TILE), lambda f: (f, 0, 0)),
                   pl.BlockSpec((FFN_TILE, d), lambda f: (f, 0))],
        out_shape=[jax.ShapeDtypeStruct((m, d), F32), jax.ShapeDtypeStruct((m, d), BF16),
                   jax.ShapeDtypeStruct((n_f, d, 2 * FFN_TILE), BF16),
                   jax.ShapeDtypeStruct((d_ff, d), BF16)],
        scratch_shapes=[pltpu.VMEM((m, d), F32)],
        compiler_params=_params("arbitrary"),
        name="ffn_and_cast",
    )(xn, x, w_gate, w_up, w_down, g.reshape(1, d), g_next.reshape(1, d))


def _cumsum_body(x_ref, o_ref):
    rows, n = x_ref.shape
    u = _tri(LANES, strict_lower=False)
    carry = jnp.zeros((rows, 1), F32)
    for c in range(n // LANES):
        sl = slice(c * LANES, (c + 1) * LANES)
        cs = _exact_dot(x_ref[:, sl], u) + carry
        o_ref[:, sl] = cs
        carry = cs[:, LANES - 1 :]


def _cumsum_lanes(x):
    return pl.pallas_call(
        _cumsum_body,
        out_shape=jax.ShapeDtypeStruct(x.shape, F32),
        name="cumsum_lanes",
    )(x)


def _flash_body(q_ref, k_ref, v_ref, c_ref, o_ref, m_ref, l_ref, acc_ref, *, tq, n_hh):
    qi = pl.program_id(2)
    m_ref[...] = jnp.full(m_ref.shape, NEG_INF, F32)
    l_ref[...] = jnp.zeros(l_ref.shape, F32)
    acc_ref[...] = jnp.zeros(acc_ref.shape, F32)
    reps = tq // LANES

    def tile(j, masked):
        start = pl.multiple_of(j * tq, tq)
        for hh in range(n_hh):
            cols = slice(hh * HEAD_DIM, (hh + 1) * HEAD_DIM)
            k = k_ref[0, pl.ds(start, tq), cols]
            v = v_ref[0, pl.ds(start, tq), cols]
            s = _nt_dot(q_ref[0, :, cols], k) - c_ref[0, hh, pl.ds(j, 1), :] * LOG2_E
            if masked:
                row = lax.broadcasted_iota(jnp.int32, (tq, tq), 0)
                col = lax.broadcasted_iota(jnp.int32, (tq, tq), 1)
                s = jnp.where(col <= row, s, NEG_INF)
            m_prev = m_ref[hh]
            m_new = jnp.maximum(m_prev, jnp.max(s, axis=-1, keepdims=True))
            alpha = jnp.exp2(m_prev - m_new)
            p = jnp.exp2(s - jnp.tile(m_new, (1, reps))).astype(BF16)
            pv = jnp.dot(p, jnp.concatenate([v, jnp.ones_like(v)], axis=1), preferred_element_type=F32)
            acc_ref[hh] = alpha * acc_ref[hh] + pv[:, :HEAD_DIM]
            l_ref[hh] = alpha * l_ref[hh] + pv[:, HEAD_DIM:]
            m_ref[hh] = m_new

    def body(j, carry):
        tile(j, False)
        return carry

    lax.fori_loop(0, qi, body, 0)
    tile(qi, True)
    for hh in range(n_hh):
        cols = slice(hh * HEAD_DIM, (hh + 1) * HEAD_DIM)
        o_ref[0, :, cols] = (acc_ref[hh] / l_ref[hh]).astype(o_ref.dtype)


def _fox_prompt_attention(q, k, v, c, tq=512, n_hh=4):
    b, s, d = q.shape
    h = d // HEAD_DIM
    n_q = s // tq
    c4 = c.reshape(b, h, n_q, tq)
    w = n_hh * HEAD_DIM
    return pl.pallas_call(
        functools.partial(_flash_body, tq=tq, n_hh=n_hh),
        grid=(b, h // n_hh, n_q),
        in_specs=[
            pl.BlockSpec((1, tq, w), lambda bi, hi, qi: (bi, qi, hi)),
            pl.BlockSpec((1, s, w), lambda bi, hi, qi: (bi, 0, hi)),
            pl.BlockSpec((1, s, w), lambda bi, hi, qi: (bi, 0, hi)),
            pl.BlockSpec((1, n_hh, n_q, tq), lambda bi, hi, qi: (bi, hi, 0, 0)),
        ],
        out_specs=pl.BlockSpec((1, tq, w), lambda bi, hi, qi: (bi, qi, hi)),
        out_shape=jax.ShapeDtypeStruct((b, s, d), BF16),
        scratch_shapes=[
            pltpu.VMEM((n_hh, tq, LANES), F32),
            pltpu.VMEM((n_hh, tq, LANES), F32),
            pltpu.VMEM((n_hh, tq, HEAD_DIM), F32),
        ],
        compiler_params=_params("parallel", "parallel", "arbitrary"),
        name="fox_prompt_attention",
    )(q, k, v, c4)


def _lane_scan(x, lane, step, limit=LANES, reverse=False, cyclic=False):
    shift = step
    while shift < limit:
        if cyclic:
            x = x + pltpu.roll(x, shift, axis=1)
        elif reverse:
            x = x + jnp.where(lane + shift < limit, pltpu.roll(x, LANES - shift, axis=1), 0.0)
        else:
            x = x + jnp.where(lane >= shift, pltpu.roll(x, shift, axis=1), 0.0)
        shift *= 2
    return x


def _sublane_suffix(x, row, cyclic=False):
    shift = 1
    while shift < SUBLANES:
        rolled = pltpu.roll(x, SUBLANES - shift, axis=0)
        x = x + (rolled if cyclic else jnp.where(row + shift < SUBLANES, rolled, 0.0))
        shift *= 2
    return x


def _decode_body(pt_ref, q_ref, *refs, n_pp, n_groups, n_tok, n_hh):
    del pt_ref
    k_refs, v_refs, lf_refs = refs[:n_pp], refs[n_pp : 2 * n_pp], refs[2 * n_pp : 3 * n_pp]
    kn_ref, vn_ref, lfn_ref, o_ref, bias_ref, m_ref, l_ref, acc_ref, carry_ref = refs[3 * n_pp :]
    p = pl.program_id(1)
    n_halves = q_ref.shape[1]
    n_rows = n_tok * n_hh
    lane = lax.broadcasted_iota(jnp.int32, (SUBLANES, LANES), 1)
    sub = lax.broadcasted_iota(jnp.int32, (SUBLANES, LANES), 0)
    row_q = lax.broadcasted_iota(jnp.int32, (n_rows, LANES), 0)
    lane_q = lax.broadcasted_iota(jnp.int32, (n_rows, LANES), 1)
    same_head = row_q % n_hh == lane_q % n_hh
    head_mask = jnp.where(same_head, 0.0, NEG_INF)

    @pl.when(p == 0)
    def _():
        bias_ref[...] = jnp.full(bias_ref.shape, NEG_INF, F32)
        m_ref[...] = jnp.full(m_ref.shape, 0.5 * NEG_INF, F32)
        l_ref[...] = jnp.zeros(l_ref.shape, F32)
        acc_ref[...] = jnp.zeros(acc_ref.shape, F32)
        carry_ref[...] = jnp.zeros(carry_ref.shape, F32)

    def attend(half, kbs, vbs, biases):
        q = q_ref[0, half]
        ss = []
        for kb, bias in zip(kbs, biases):
            s = _nt_dot(q, kb)
            ss.append(jnp.concatenate(
                [s[:, a * LANES : (a + 1) * LANES] + bias_a for a, bias_a in enumerate(bias)], axis=1))
        m_prev = m_ref[half]
        m_new = m_prev
        for s in ss:
            m_new = jnp.maximum(m_new, jnp.max(s, axis=-1, keepdims=True))
        alpha = jnp.exp2(m_prev - m_new)
        l_new = alpha * l_ref[half]
        acc = alpha * acc_ref[half]
        for s, vb in zip(ss, vbs):
            pr = jnp.exp2(s - jnp.tile(m_new, (1, s.shape[1] // LANES)))
            l_new = l_new + jnp.sum(pr, axis=-1, keepdims=True)
            acc = acc + jnp.dot(pr.astype(BF16), vb, preferred_element_type=F32)
        l_ref[half] = l_new
        acc_ref[half] = acc
        m_ref[half] = m_new

    def half_rows(ref, half):
        x = ref[:, half]
        return x.reshape(x.shape[0] * x.shape[1], x.shape[2]).astype(BF16)

    for half in range(n_halves):
        attend(half,
               [half_rows(k_ref, half) for k_ref in k_refs],
               [half_rows(v_ref, half) for v_ref in v_refs],
               [[head_mask + bias_ref[i, half, a : a + 1, :] for a in range(SUBLANES)] for i in range(n_pp)])

    for half in range(n_halves):
        carry = carry_ref[half]
        for i, lf_ref in enumerate(lf_refs):
            lf = lf_ref[0, half] * LOG2_E
            later = jnp.where(lane + n_hh < LANES, pltpu.roll(lf, LANES - n_hh, axis=1), 0.0)
            in_row = _lane_scan(later, lane, n_hh, reverse=True)
            row_total = _lane_scan(lf, lane, n_hh, cyclic=True)
            below = jnp.where(sub + 1 < SUBLANES, pltpu.roll(row_total, SUBLANES - 1, axis=0), 0.0)
            bias_ref[i, half] = in_row + _sublane_suffix(below, sub) + carry
            carry = carry + _sublane_suffix(row_total, sub, cyclic=True)
        carry_ref[half] = carry

    @pl.when(p == n_groups)
    def _():
        for half in range(n_halves):
            c_new = _lane_scan(jnp.broadcast_to(lfn_ref[0, half], (SUBLANES, LANES)) * LOG2_E, lane, n_hh,
                               limit=n_rows)
            causal = same_head & (lane_q // n_hh <= row_q // n_hh)
            attend(half, [kn_ref[0, half]], [vn_ref[0, half]], [[jnp.where(causal, -c_new[0:1, :], NEG_INF)]])
            o_ref[0, half] = acc_ref[half] / l_ref[half]


def _fox_sample_attention(page_table, q, k_cache, v_cache, lf_tiles, k_new, v_new, lf_new, base, n_pp=8):
    b, n_halves, n_rows, hd = q.shape
    n_hh = k_cache.shape[2]
    n_pages = page_table.shape[1]
    page = k_cache.shape[0] // lf_tiles.shape[0]
    assert n_hh == SUBLANES and page * n_hh == SUBLANES * LANES and n_pages % n_pp == 0
    n_groups = n_pages // n_pp

    def page_of(bi, group, pt, slot):
        return base + pt[bi, n_pages - 1 - (group * n_pp + slot)]

    def page_kv(bi, pi, pt, *, slot):
        return (page_of(bi, jnp.maximum(pi - 1, 0), pt, slot), 0, 0, 0)

    def page_lf(bi, pi, pt, *, slot):
        return (page_of(bi, jnp.minimum(pi, n_groups - 1), pt, slot), 0, 0, 0)

    def per_seq(bi, pi, pt):
        return (bi, 0, 0, 0)

    slots = range(n_pp)
    kv_specs = [pl.BlockSpec((page, n_halves, n_hh, hd), functools.partial(page_kv, slot=i)) for i in slots]
    lf_specs = [pl.BlockSpec((1, n_halves, SUBLANES, LANES), functools.partial(page_lf, slot=i))
                for i in slots]
    state = pltpu.VMEM((n_halves, n_rows, LANES), F32)
    grid_spec = pltpu.PrefetchScalarGridSpec(
        num_scalar_prefetch=1,
        grid=(b, n_groups + 1),
        in_specs=[pl.BlockSpec((1, n_halves, n_rows, hd), per_seq), *kv_specs, *kv_specs, *lf_specs,
                  pl.BlockSpec((1, n_halves, LANES, hd), per_seq),
                  pl.BlockSpec((1, n_halves, LANES, hd), per_seq),
                  pl.BlockSpec((1, n_halves, 1, LANES), per_seq)],
        out_specs=pl.BlockSpec((1, n_halves, n_rows, hd), per_seq),
        scratch_shapes=[
            pltpu.VMEM((n_pp, n_halves, SUBLANES, LANES), F32),
            state, state, state,
            pltpu.VMEM((n_halves, SUBLANES, LANES), F32),
        ],
    )
    return pl.pallas_call(
        functools.partial(_decode_body, n_pp=n_pp, n_groups=n_groups, n_tok=n_rows // n_hh, n_hh=n_hh),
        grid_spec=grid_spec,
        out_shape=jax.ShapeDtypeStruct((b, n_halves, n_rows, hd), F32),
        compiler_params=_params("parallel", "arbitrary"),
        name="fox_sample_attention",
    )(page_table, q, *[k_cache] * n_pp, *[v_cache] * n_pp, *[lf_tiles] * n_pp, k_new, v_new, lf_new)


def _gate_windows(d_rnn, block_w):
    spans = []
    for c0 in range(0, d_rnn, GATE_TILE):
        n_lo, n_hi = c0 // block_w, (c0 + GATE_TILE - 1) // block_w
        spans.append((block_w * n_lo // LANES * LANES, block_w * (n_hi + 1)))
    win = max(-(-(hi - lo) // LANES) * LANES for lo, hi in spans)
    starts = [min(lo, d_rnn - win) for lo, _ in spans]
    assert all(s + win >= hi for s, (_, hi) in zip(starts, spans))
    return starts, win


def _pack_block_diag(w, starts, win):
    n_blocks, bw, _ = w.shape
    tiles = []
    for c, k0 in enumerate(starts):
        c0, c1 = c * GATE_TILE, (c + 1) * GATE_TILE
        tile = jnp.zeros((win, GATE_TILE), w.dtype)
        for n in range(c0 // bw, (c1 - 1) // bw + 1):
            g0, g1 = max(bw * n, c0), min(bw * (n + 1), c1)
            piece = w[n][:, g0 - bw * n : g1 - bw * n]
            r0 = bw * n - k0
            tile = tile + jnp.pad(piece, ((r0, win - r0 - bw), (g0 - c0, c1 - g1)))
        tiles.append(tile)
    return jnp.stack(tiles)


def _lru_body(rec_ref, gate_ref, cinit_ref, hinit_ref, cw_ref, cb_ref, wa_ref, wi_ref, ba_ref, bi_ref,
              lam_ref, y_ref, hlast_ref, prev_ref, h_ref, *, tm, n_valid, starts, win):
    @pl.when(pl.program_id(1) == 0)
    def _():
        prev_ref[...] = cinit_ref[0]
        h_ref[...] = hinit_ref[0]

    x = rec_ref[0]
    d_rnn = x.shape[1]
    prev = prev_ref[...]
    row8 = lax.broadcasted_iota(jnp.int32, (SUBLANES, 1), 0)
    row = lax.broadcasted_iota(jnp.int32, (tm, 1), 0)
    row_in_group = lax.broadcasted_iota(jnp.int32, (1, SUBLANES, 1), 1)

    def delayed(k):
        r = pltpu.roll(x, k, axis=0)
        head = jnp.where(row8 < k, pltpu.roll(prev, k, axis=0), r[:SUBLANES])
        return head if tm == SUBLANES else jnp.concatenate([head, r[SUBLANES:]], axis=0)

    conv = cb_ref[...] + delayed(3) * cw_ref[0:1, :]
    conv = conv + delayed(2) * cw_ref[1:2, :]
    conv = conv + delayed(1) * cw_ref[2:3, :]
    conv = conv + x * cw_ref[3:4, :]
    prev_ref[...] = x[tm - SUBLANES :, :]
    conv_bf = conv.astype(BF16)
    sp = _softplus(-lam_ref[...])
    h_prev = h_ref[...]

    h_last = []
    for c, k0 in enumerate(starts):
        sl = slice(c * GATE_TILE, (c + 1) * GATE_TILE)
        window = conv_bf[:, k0 : k0 + win]
        r = jax.nn.sigmoid(jnp.dot(window, wa_ref[c], preferred_element_type=F32) + ba_ref[:, sl])
        ig = jax.nn.sigmoid(jnp.dot(window, wi_ref[c], preferred_element_type=F32) + bi_ref[:, sl])
        log_a = (-C_RG * r) * sp[:, sl]
        a = jnp.exp(log_a)
        xin = (jnp.sqrt(-(jnp.tanh(log_a) * (a * a + 1.0))) * ig) * conv[:, sl]
        if n_valid < tm:
            a = jnp.where(row < n_valid, a, 1.0)
            xin = jnp.where(row < n_valid, xin, 0.0)
        a = a.reshape(tm // SUBLANES, SUBLANES, GATE_TILE)
        xin = xin.reshape(tm // SUBLANES, SUBLANES, GATE_TILE)
        shift = 1
        while shift < SUBLANES:
            a_sh = jnp.where(row_in_group < shift, 1.0, pltpu.roll(a, shift, axis=1))
            x_sh = jnp.where(row_in_group < shift, 0.0, pltpu.roll(xin, shift, axis=1))
            xin = a * x_sh + xin
            a = a * a_sh
            shift *= 2
        h_in = h_prev[0:1, sl]
        groups = []
        for g in range(tm // SUBLANES):
            h_g = a[g] * h_in + xin[g]
            groups.append(h_g)
            h_in = h_g[SUBLANES - 1 :, :]
        h = groups[0] if len(groups) == 1 else jnp.concatenate(groups, axis=0)
        y_ref[0, :, sl] = (h * _gelu_tanh(gate_ref[0, :, sl])).astype(y_ref.dtype)
        h_last.append(h_in)
    h_new = jnp.broadcast_to(jnp.concatenate(h_last, axis=1), (SUBLANES, d_rnn))
    h_ref[...] = h_new
    hlast_ref[0] = h_new


def _rglru(u, conv_init, h_init, conv_w, conv_b, wa_pack, wi_pack, b_a, b_i, lam,
           starts, win, tm, n_valid):
    b, t, r = u.shape[0], u.shape[1], u.shape[2] // 2
    n_tiles = wa_pack.shape[0]

    def const2(bi, ti):
        return (0, 0)

    def const3(bi, ti):
        return (0, 0, 0)

    def per_seq(bi, ti):
        return (bi, 0, 0)

    def tile(bi, ti):
        return (bi, ti, 0)

    return pl.pallas_call(
        functools.partial(_lru_body, tm=tm, n_valid=n_valid, starts=tuple(starts), win=win),
        grid=(b, t // tm),
        in_specs=[
            pl.BlockSpec((1, tm, r), lambda bi, ti: (bi, ti, 1)),
            pl.BlockSpec((1, tm, r), tile),
            pl.BlockSpec((1, SUBLANES, r), per_seq),
            pl.BlockSpec((1, SUBLANES, r), per_seq),
            pl.BlockSpec((SUBLANES, r), const2),
            pl.BlockSpec((1, r), const2),
            pl.BlockSpec((n_tiles, win, GATE_TILE), const3),
            pl.BlockSpec((n_tiles, win, GATE_TILE), const3),
            pl.BlockSpec((1, r), const2),
            pl.BlockSpec((1, r), const2),
            pl.BlockSpec((1, r), const2),
        ],
        out_specs=[
            pl.BlockSpec((1, tm, r), tile),
            pl.BlockSpec((1, SUBLANES, r), per_seq),
        ],
        out_shape=[
            jax.ShapeDtypeStruct((b, t, r), BF16),
            jax.ShapeDtypeStruct((b, SUBLANES, r), F32),
        ],
        scratch_shapes=[pltpu.VMEM((SUBLANES, r), F32), pltpu.VMEM((SUBLANES, r), F32)],
        compiler_params=_params("parallel", "arbitrary"),
        name="rglru",
    )(u, u, conv_init, h_init, conv_w, conv_b, wa_pack, wi_pack, b_a, b_i, lam)


def _fox_project(xn, w_qkv, w_f, b_f, layer, d, k_stack, v_stack):
    n_layers = w_qkv.shape[0]
    q, = _matmul(xn, w_qkv, layer, 0, d, 1024, _epi_store_query, [BF16])
    k_stack, k16 = _kv_projection(xn, w_qkv, layer, d, k_stack, n_layers)
    v_stack, v16 = _kv_projection(xn, w_qkv, layer, 2 * d, v_stack, n_layers)
    lf, = _matmul(xn, w_f, layer, 0, LANES, LANES, _epi_log_sigmoid, [F32], bias=b_f)
    return q, k16, v16, lf, k_stack, v_stack


def kernel(x_prompt, x_sample, cache_k, cache_v, cache_logf, state_conv, state_h, page_table,
           norm_mix_pre, norm_mix_post, norm_ffn_pre, norm_ffn_post,
           fox_w_qkv, fox_w_f, fox_b_f, fox_w_o,
           lru_w_in, lru_conv_w, lru_conv_b, lru_w_a, lru_b_a, lru_w_i, lru_b_i, lru_lam, lru_w_out,
           ffn_w_gate, ffn_w_up, ffn_w_down):
    batch, seq, d = x_prompt.shape
    dec_batch, dec_seq, _ = x_sample.shape
    depth = norm_mix_pre.shape[0]
    n_heads = d // HEAD_DIM
    n_fox, n_pool, page = cache_logf.shape[:3]
    d_rnn = lru_lam.shape[1]
    block_w = lru_w_a.shape[2]
    m_p = batch * seq
    m_s = dec_batch * SAMPLE_ROWS
    pad_rows = SAMPLE_ROWS - dec_seq

    xp = x_prompt.reshape(m_p, d)
    xs = jnp.pad(x_sample, ((0, 0), (0, pad_rows), (0, 0))).reshape(m_s, d)

    halves = n_heads // SUBLANES
    k_cache = cache_k.reshape(n_fox * n_pool * page, halves, SUBLANES, HEAD_DIM)
    v_cache = cache_v.reshape(n_fox * n_pool * page, halves, SUBLANES, HEAD_DIM)
    lf_tiles = cache_logf.reshape(n_fox * n_pool, page, halves, SUBLANES).transpose(0, 2, 1, 3)
    lf_tiles = lf_tiles.reshape(n_fox * n_pool, halves, SUBLANES, LANES)
    starts, win = _gate_windows(d_rnn, block_w)

    w_qkv = fox_w_qkv.astype(BF16)
    w_f = jnp.pad(fox_w_f, ((0, 0), (0, 0), (0, LANES - n_heads))).astype(BF16)
    w_o = fox_w_o.astype(BF16)
    w_in = lru_w_in.astype(BF16)
    w_out = lru_w_out.astype(BF16)

    lp_l, ls_l = [], []
    cp_l, hp_l, cs_l, hs_l = [], [], [], []
    kp_stack = vp_stack = ks_stack = vs_stack = None
    xn_p = _rms_norm(xp, norm_mix_pre[0])
    xn_s = _rms_norm(xs, norm_mix_pre[0])
    for i in range(depth):
        j = i // 2
        g_post, g_ffn = norm_mix_post[i], norm_ffn_pre[i]
        g_next = norm_mix_pre[min(i + 1, depth - 1)]
        if i % 2 == 0:
            b_f = jnp.pad(fox_b_f[j], (0, LANES - n_heads)).reshape(1, LANES)

            q, k16, v16, lf, kp_stack, vp_stack = _fox_project(xn_p, w_qkv, w_f, b_f, j, d, kp_stack, vp_stack)
            lf = lf[:, :n_heads].reshape(batch, seq, n_heads)
            c = _cumsum_lanes(lf.transpose(0, 2, 1).reshape(batch * n_heads, seq))
            o = _fox_prompt_attention(q.reshape(batch, seq, d), k16.reshape(batch, seq, d),
                                      v16.reshape(batch, seq, d), c.reshape(batch, n_heads, seq))
            xp, xn_p = _matmul_post_norm(o.reshape(m_p, d), w_o, j, g_post, g_ffn, xp)
            lp_l.append(lf)

            q, k16, v16, lf, ks_stack, vs_stack = _fox_project(xn_s, w_qkv, w_f, b_f, j, d, ks_stack, vs_stack)
            lf = lf[:, :n_heads].reshape(dec_batch, SAMPLE_ROWS, n_heads)[:, :dec_seq]
            def split_heads(x, rows):
                x = x.reshape(dec_batch, SAMPLE_ROWS, halves, SUBLANES, HEAD_DIM)[:, :rows]
                return x.transpose(0, 2, 1, 3, 4).reshape(dec_batch, halves, rows * SUBLANES, HEAD_DIM)

            pad_new = ((0, 0), (0, 0), (0, LANES - SAMPLE_ROWS * SUBLANES), (0, 0))
            lf_new = jnp.pad(lf, ((0, 0), (0, pad_rows), (0, 0)))
            lf_new = lf_new.reshape(dec_batch, SAMPLE_ROWS, halves, SUBLANES).transpose(0, 2, 1, 3)
            lf_new = jnp.pad(lf_new.reshape(dec_batch, halves, 1, SAMPLE_ROWS * SUBLANES),
                             ((0, 0), (0, 0), (0, 0), (0, LANES - SAMPLE_ROWS * SUBLANES)))
            o = _fox_sample_attention(
                page_table, split_heads(q, dec_seq), k_cache, v_cache, lf_tiles,
                jnp.pad(split_heads(k16, SAMPLE_ROWS), pad_new),
                jnp.pad(split_heads(v16, SAMPLE_ROWS), pad_new),
                lf_new, j * n_pool)
            o = o.reshape(dec_batch, halves, dec_seq, SUBLANES, HEAD_DIM).transpose(0, 2, 1, 3, 4)
            o = jnp.pad(o.reshape(dec_batch, dec_seq, d), ((0, 0), (0, pad_rows), (0, 0)))
            xs, xn_s = _matmul_post_norm(o.reshape(m_s, d).astype(BF16), w_o, j, g_post, g_ffn, xs)
            ls_l.append(lf)
        else:
            conv_w = jnp.pad(lru_conv_w[j], ((0, SUBLANES - CONV_W), (0, 0)))
            conv_b = lru_conv_b[j].reshape(1, d_rnn)
            wa_pack = _pack_block_diag(lru_w_a[j], starts, win).astype(BF16)
            wi_pack = _pack_block_diag(lru_w_i[j], starts, win).astype(BF16)
            b_a = lru_b_a[j].reshape(1, d_rnn)
            b_i = lru_b_i[j].reshape(1, d_rnn)
            lam = lru_lam[j].reshape(1, d_rnn)

            def mixer(xn, n_seq, t, conv_init, h_init, tm, n_valid):
                u, = _matmul(xn, w_in, j, 0, 2 * d_rnn, d_rnn // 2, _epi_store, [F32])
                u = u.reshape(n_seq, t, 2 * d_rnn)
                y, h_last = _rglru(u, conv_init, h_init, conv_w, conv_b,
                                   wa_pack, wi_pack, b_a, b_i, lam, starts, win, tm, n_valid)
                return y.reshape(n_seq * t, d_rnn), u[:, :, d_rnn:], h_last[:, 0]

            zeros = jnp.zeros((batch, SUBLANES, d_rnn), F32)
            y, rec, h_last = mixer(xn_p, batch, seq, zeros, zeros, 256, 256)
            xp, xn_p = _matmul_post_norm(y, w_out, j, g_post, g_ffn, xp)
            cp_l.append(rec[:, seq - (CONV_W - 1) :])
            hp_l.append(h_last)

            conv_init = jnp.pad(state_conv[j], ((0, 0), (SUBLANES - (CONV_W - 1), 0), (0, 0)))
            h_init = jnp.pad(state_h[j][:, None, :], ((0, 0), (0, SUBLANES - 1), (0, 0)))
            y, rec, h_last = mixer(xn_s, dec_batch, SAMPLE_ROWS, conv_init, h_init, SAMPLE_ROWS, dec_seq)
            xs, xn_s = _matmul_post_norm(y, w_out, j, g_post, g_ffn, xs)
            full = jnp.concatenate([state_conv[j], rec[:, :dec_seq]], axis=1)
            cs_l.append(full[:, dec_seq:])
            hs_l.append(h_last)

        xs, xn_s, w_gate_up, w_down = _ffn_and_cast(xn_s, xs, ffn_w_gate, ffn_w_up, ffn_w_down, i,
                                                    norm_ffn_post[i], g_next)
        xp, xn_p = _ffn(xn_p, xp, w_gate_up, w_down, norm_ffn_post[i], g_next)

    y_sample = xs.reshape(dec_batch, SAMPLE_ROWS, d)[:, :dec_seq]
    prompt_kv = (n_fox, batch, seq, n_heads, HEAD_DIM)
    sample_kv = (n_fox, dec_batch, SAMPLE_ROWS, n_heads, HEAD_DIM)
    return (xp.reshape(batch, seq, d), y_sample,
            kp_stack.reshape(prompt_kv), vp_stack.reshape(prompt_kv), jnp.stack(lp_l),
            ks_stack.reshape(sample_kv)[:, :, :dec_seq], vs_stack.reshape(sample_kv)[:, :, :dec_seq],
            jnp.stack(ls_l),
            jnp.stack(cp_l), jnp.stack(hp_l), jnp.stack(cs_l), jnp.stack(hs_l))
```

```python
import functools
import math

import jax
import jax.numpy as jnp
from jax import lax
from jax.experimental import pallas as pl
from jax.experimental.pallas import tpu as pltpu

F32 = jnp.float32
BF16 = jnp.bfloat16

RMS_EPS = 1e-6
NEG_INF = -1e30
LOG2_E = math.log2(math.e)
HEAD_DIM = 128
QK_SCALE = LOG2_E / math.sqrt(HEAD_DIM)
C_RG = 8.0
CONV_W = 4

LANES = 128
SUBLANES = 8
MXU_COLS = 256
VMEM_LIMIT_BYTES = 56 * 1024 * 1024

SAMPLE_ROWS = SUBLANES
GATE_TILE = MXU_COLS


def _params(*semantics):
    return pltpu.CompilerParams(dimension_semantics=semantics, vmem_limit_bytes=VMEM_LIMIT_BYTES)


def _nt_dot(a, b):
    return lax.dot_general(a, b, (((1,), (1,)), ((), ())), preferred_element_type=F32)


def _exact_dot(x, u):
    hi = x.astype(BF16)
    r1 = x - hi.astype(F32)
    mid = r1.astype(BF16)
    lo = (r1 - mid.astype(F32)).astype(BF16)
    dot = functools.partial(jnp.dot, preferred_element_type=F32)
    return dot(hi, u) + dot(mid, u) + dot(lo, u)


def _tri(n, strict_lower):
    rp = lax.broadcasted_iota(jnp.int32, (n, n), 0)
    r = lax.broadcasted_iota(jnp.int32, (n, n), 1)
    return jnp.where(rp > r if strict_lower else rp <= r, 1.0, 0.0).astype(BF16)


def _log_sigmoid(z):
    return -(jnp.maximum(-z, 0.0) + jnp.log1p(jnp.exp(-jnp.abs(z))))


def _softplus(z):
    return jnp.maximum(z, 0.0) + jnp.log1p(jnp.exp(-jnp.abs(z)))


def _gelu_tanh(x):
    cdf = 0.5 * (1.0 + jnp.tanh(math.sqrt(2.0 / math.pi) * (x + 0.044715 * (x * x * x))))
    return x * cdf


NORM_CHUNK = 512


def _rms_norm_body(x_ref, g_ref, o_ref):
    x = x_ref[...]
    var = jnp.mean(x * x, axis=-1, keepdims=True)
    o_ref[...] = ((x * lax.rsqrt(var + RMS_EPS)) * g_ref[...]).astype(o_ref.dtype)


def _rms_norm(x, g, tm=512):
    m, d = x.shape
    tm = min(tm, m)
    return pl.pallas_call(
        _rms_norm_body,
        grid=(m // tm,),
        in_specs=[pl.BlockSpec((tm, d), lambda i: (i, 0)), pl.BlockSpec((1, d), lambda i: (0, 0))],
        out_specs=pl.BlockSpec((tm, d), lambda i: (i, 0)),
        out_shape=jax.ShapeDtypeStruct((m, d), BF16),
        compiler_params=_params("parallel"),
        name="rms_norm",
    )(x, g.reshape(1, d))


def _residual_norm_epilogue(y_chunk, res_ref, g_ref, gn_ref, o_ref, xn_ref, d):
    n_c = d // NORM_CHUNK
    chunks = [slice(c * NORM_CHUNK, (c + 1) * NORM_CHUNK) for c in range(n_c)]

    def sum_sq(get):
        total = None
        for c in range(n_c):
            v = get(c)
            part = jnp.sum(v * v, axis=-1, keepdims=True)
            total = part if total is None else total + part
        return total

    inv = lax.rsqrt(sum_sq(y_chunk) / d + RMS_EPS)
    for c, sl in enumerate(chunks):
        o_ref[:, sl] = res_ref[:, sl] + (y_chunk(c) * inv) * g_ref[:, sl]
    inv_n = lax.rsqrt(sum_sq(lambda c: o_ref[:, chunks[c]]) / d + RMS_EPS)
    for sl in chunks:
        xn_ref[:, sl] = ((o_ref[:, sl] * inv_n) * gn_ref[:, sl]).astype(xn_ref.dtype)


def _epi_store(acc, o_ref):
    o_ref[...] = acc.astype(o_ref.dtype)


def _epi_store_query(acc, o_ref):
    o_ref[...] = (acc * QK_SCALE).astype(o_ref.dtype)


def _epi_log_sigmoid(acc, b_ref, o_ref):
    o_ref[...] = _log_sigmoid(acc + b_ref[...])


def _mm_body(x_ref, w_ref, *rest, epilogue):
    epilogue(jnp.dot(x_ref[...], w_ref[...], preferred_element_type=F32), *rest)


def _matmul(x, w, layer, col_off, n_cols, tn, epilogue, out_dtypes, bias=None, tm=1024):
    m, k = x.shape
    tm = min(tm, m)
    assert col_off % tn == 0
    off = col_off // tn
    in_specs = [
        pl.BlockSpec((tm, k), lambda i, j: (i, 0)),
        pl.BlockSpec((None, k, tn), lambda i, j: (layer, 0, j + off)),
    ]
    args = [x, w]
    if bias is not None:
        in_specs.append(pl.BlockSpec((1, tn), lambda i, j: (0, j)))
        args.append(bias)
    return pl.pallas_call(
        functools.partial(_mm_body, epilogue=epilogue),
        grid=(m // tm, n_cols // tn),
        in_specs=in_specs,
        out_specs=[pl.BlockSpec((tm, tn), lambda i, j: (i, j)) for _ in out_dtypes],
        out_shape=[jax.ShapeDtypeStruct((m, n_cols), dt) for dt in out_dtypes],
        compiler_params=_params("parallel", "arbitrary"),
        name="mm_" + epilogue.__name__.removeprefix("_epi_"),
    )(*args)


def _kv_body(x_ref, w_ref, *refs):
    o32_ref, o16_ref = refs[-2:]
    acc = jnp.dot(x_ref[...], w_ref[...], preferred_element_type=F32)
    o32_ref[...] = acc.reshape(o32_ref.shape)
    o16_ref[...] = acc.astype(BF16)


def _kv_projection(x, w, layer, col_off, stacked, n_layers, tm=1024, tn=1024):
    m, k = x.shape
    d = w.shape[2] // 3
    tm = min(tm, m)
    n_heads, hpt = d // HEAD_DIM, tn // HEAD_DIM
    off, row0 = col_off // tn, layer * (m // tm)
    in_specs = [
        pl.BlockSpec((tm, k), lambda i, j: (i, 0)),
        pl.BlockSpec((None, k, tn), lambda i, j: (layer, 0, j + off)),
    ]
    args = [x, w]
    if stacked is not None:
        in_specs.append(pl.BlockSpec(memory_space=pl.ANY))
        args.append(stacked)
    return pl.pallas_call(
        _kv_body,
        grid=(m // tm, d // tn),
        in_specs=in_specs,
        out_specs=[pl.BlockSpec((tm, hpt, HEAD_DIM), lambda i, j: (row0 + i, j, 0)),
                   pl.BlockSpec((tm, tn), lambda i, j: (i, j))],
        out_shape=[jax.ShapeDtypeStruct((n_layers * m, n_heads, HEAD_DIM), F32),
                   jax.ShapeDtypeStruct((m, d), BF16)],
        input_output_aliases={} if stacked is None else {2: 0},
        compiler_params=_params("parallel", "arbitrary"),
        name="kv_projection",
    )(*args)


def _mm_post_body(a_ref, w_ref, g_ref, gn_ref, res_ref, o_ref, xn_ref, y_ref, *, n_j, d):
    j = pl.program_id(1)
    y_ref[j] = jnp.dot(a_ref[...], w_ref[...], preferred_element_type=F32)
    per_j = y_ref.shape[2] // NORM_CHUNK

    def y_chunk(c):
        lo = (c % per_j) * NORM_CHUNK
        return y_ref[c // per_j, :, lo : lo + NORM_CHUNK]

    @pl.when(j == n_j - 1)
    def _():
        _residual_norm_epilogue(y_chunk, res_ref, g_ref, gn_ref, o_ref, xn_ref, d)


def _matmul_post_norm(a, w, layer, g, g_next, res, tm=512, tn=1024):
    m, k = a.shape
    d = w.shape[2]
    tm = min(tm, m)
    n_j = d // tn
    row = pl.BlockSpec((1, d), lambda i, j: (0, 0))
    tile = pl.BlockSpec((tm, d), lambda i, j: (i, 0))
    return pl.pallas_call(
        functools.partial(_mm_post_body, n_j=n_j, d=d),
        grid=(m // tm, n_j),
        in_specs=[
            pl.BlockSpec((tm, k), lambda i, j: (i, 0)),
            pl.BlockSpec((None, k, tn), lambda i, j: (layer, 0, j)),
            row, row, tile,
        ],
        out_specs=[tile, tile],
        out_shape=[jax.ShapeDtypeStruct((m, d), F32), jax.ShapeDtypeStruct((m, d), BF16)],
        scratch_shapes=[pltpu.VMEM((n_j, tm, tn), F32)],
        compiler_params=_params("parallel", "arbitrary"),
        name="mm_post_norm",
    )(a, w, g.reshape(1, d), g_next.reshape(1, d), res)


FFN_TILE = 512


def _ffn_step(f, n_f, xn_ref, x_ref, w_gate_up, w_down, g_ref, gn_ref, o_ref, xn_out_ref, acc_ref):
    d = acc_ref.shape[1]

    @pl.when(f == 0)
    def _():
        acc_ref[...] = jnp.zeros(acc_ref.shape, F32)

    gate_up = jnp.dot(xn_ref[...], w_gate_up[...], preferred_element_type=F32)
    gate, up = gate_up[:, :FFN_TILE], gate_up[:, FFN_TILE:]
    h = ((gate * jax.nn.sigmoid(gate)) * up).astype(BF16)
    for c in range(d // NORM_CHUNK):
        sl = slice(c * NORM_CHUNK, (c + 1) * NORM_CHUNK)
        acc_ref[:, sl] += jnp.dot(h, w_down[:, sl], preferred_element_type=F32)

    @pl.when(f == n_f - 1)
    def _():
        _residual_norm_epilogue(lambda c: acc_ref[:, c * NORM_CHUNK : (c + 1) * NORM_CHUNK],
                                x_ref, g_ref, gn_ref, o_ref, xn_out_ref, d)


def _ffn_body(xn_ref, x_ref, wgu_ref, wd_ref, g_ref, gn_ref, o_ref, xn_out_ref, acc_ref, *, n_f):
    _ffn_step(pl.program_id(1), n_f, xn_ref, x_ref, wgu_ref, wd_ref, g_ref, gn_ref, o_ref, xn_out_ref, acc_ref)


def _ffn(xn, x, w_gate_up, w_down, g, g_next, tm=512):
    m, d = x.shape
    n_f = w_gate_up.shape[0]
    row = pl.BlockSpec((1, d), lambda i, f: (0, 0))
    tile = pl.BlockSpec((tm, d), lambda i, f: (i, 0))
    return pl.pallas_call(
        functools.partial(_ffn_body, n_f=n_f),
        grid=(m // tm, n_f),
        in_specs=[
            tile, tile,
            pl.BlockSpec((None, d, 2 * FFN_TILE), lambda i, f: (f, 0, 0)),
            pl.BlockSpec((FFN_TILE, d), lambda i, f: (f, 0)),
            row, row,
        ],
        out_specs=[tile, tile],
        out_shape=[jax.ShapeDtypeStruct((m, d), F32), jax.ShapeDtypeStruct((m, d), BF16)],
        scratch_shapes=[pltpu.VMEM((tm, d), F32)],
        compiler_params=_params("parallel", "arbitrary"),
        name="ffn",
    )(xn, x, w_gate_up, w_down, g.reshape(1, d), g_next.reshape(1, d))


def _ffn_cast_body(xn_ref, x_ref, wg_ref, wu_ref, wd_ref, g_ref, gn_ref,
                   o_ref, xn_out_ref, wgu16_ref, wd16_ref, acc_ref, *, n_f):
    wgu16_ref[:, :FFN_TILE] = wg_ref[...].astype(BF16)
    wgu16_ref[:, FFN_TILE:] = wu_ref[...].astype(BF16)
    wd16_ref[...] = wd_ref[...].astype(BF16)
    _ffn_step(pl.program_id(0), n_f, xn_ref, x_ref, wgu16_ref, wd16_ref, g_ref, gn_ref, o_ref, xn_out_ref,
              acc_ref)


def _ffn_and_cast(xn, x, w_gate, w_up, w_down, layer, g, g_next):
    m, d = x.shape
    d_ff = w_gate.shape[2]
    n_f = d_ff // FFN_TILE
    row = pl.BlockSpec((1, d), lambda f: (0, 0))
    whole = pl.BlockSpec((m, d), lambda f: (0, 0))
    cols = pl.BlockSpec((None, d, FFN_TILE), lambda f: (layer, 0, f))
    return pl.pallas_call(
        functools.partial(_ffn_cast_body, n_f=n_f),
        grid=(n_f,),
        in_specs=[whole, whole, cols, cols,
                  pl.BlockSpec((None, FFN_TILE, d), lambda f: (layer, f, 0)), row, row],
        out_specs=[whole, whole,
                   pl.BlockSpec((None, d, 2 * FFN_TILE), lambda f: (f, 0, 0)),
                   pl.BlockSpec((FFN_TILE, d), lambda f: (f, 0))],
        out_shape=[jax.ShapeDtypeStruct((m, d), F32), jax.ShapeDtypeStruct((m, d), BF16),
                   jax.ShapeDtypeStruct((n_f, d, 2 * FFN_TILE), BF16),
                   jax.ShapeDtypeStruct((d_ff, d), BF16)],
        scratch_shapes=[pltpu.VMEM((m, d), F32)],
        compiler_params=_params("arbitrary"),
        name="ffn_and_cast",
    )(xn, x, w_gate, w_up, w_down, g.reshape(1, d), g_next.reshape(1, d))


def _cumsum_body(x_ref, o_ref):
    rows, n = x_ref.shape
    u = _tri(LANES, strict_lower=False)
    carry = jnp.zeros((rows, 1), F32)
    for c in range(n // LANES):
        sl = slice(c * LANES, (c + 1) * LANES)
        cs = _exact_dot(x_ref[:, sl], u) + carry
        o_ref[:, sl] = cs
        carry = cs[:, LANES - 1 :]


def _cumsum_lanes(x):
    return pl.pallas_call(
        _cumsum_body,
        out_shape=jax.ShapeDtypeStruct(x.shape, F32),
        name="cumsum_lanes",
    )(x)


def _flash_body(q_ref, k_ref, v_ref, c_ref, o_ref, m_ref, l_ref, acc_ref, *, tq, n_hh):
    qi = pl.program_id(2)
    m_ref[...] = jnp.full(m_ref.shape, NEG_INF, F32)
    l_ref[...] = jnp.zeros(l_ref.shape, F32)
    acc_ref[...] = jnp.zeros(acc_ref.shape, F32)
    reps = tq // LANES

    def tile(j, masked):
        start = pl.multiple_of(j * tq, tq)
        for hh in range(n_hh):
            cols = slice(hh * HEAD_DIM, (hh + 1) * HEAD_DIM)
            k = k_ref[0, pl.ds(start, tq), cols]
            v = v_ref[0, pl.ds(start, tq), cols]
            s = _nt_dot(q_ref[0, :, cols], k) - c_ref[0, hh, pl.ds(j, 1), :] * LOG2_E
            if masked:
                row = lax.broadcasted_iota(jnp.int32, (tq, tq), 0)
                col = lax.broadcasted_iota(jnp.int32, (tq, tq), 1)
                s = jnp.where(col <= row, s, NEG_INF)
            m_prev = m_ref[hh]
            m_new = jnp.maximum(m_prev, jnp.max(s, axis=-1, keepdims=True))
            alpha = jnp.exp2(m_prev - m_new)
            p = jnp.exp2(s - jnp.tile(m_new, (1, reps))).astype(BF16)
            pv = jnp.dot(p, jnp.concatenate([v, jnp.ones_like(v)], axis=1), preferred_element_type=F32)
            acc_ref[hh] = alpha * acc_ref[hh] + pv[:, :HEAD_DIM]
            l_ref[hh] = alpha * l_ref[hh] + pv[:, HEAD_DIM:]
            m_ref[hh] = m_new

    def body(j, carry):
        tile(j, False)
        return carry

    lax.fori_loop(0, qi, body, 0)
    tile(qi, True)
    for hh in range(n_hh):
        cols = slice(hh * HEAD_DIM, (hh + 1) * HEAD_DIM)
        o_ref[0, :, cols] = (acc_ref[hh] / l_ref[hh]).astype(o_ref.dtype)


def _fox_prompt_attention(q, k, v, c, tq=512, n_hh=8):
    b, s, d = q.shape
    h = d // HEAD_DIM
    n_q = s // tq
    c4 = c.reshape(b, h, n_q, tq)
    w = n_hh * HEAD_DIM
    return pl.pallas_call(
        functools.partial(_flash_body, tq=tq, n_hh=n_hh),
        grid=(b, h // n_hh, n_q),
        in_specs=[
            pl.BlockSpec((1, tq, w), lambda bi, hi, qi: (bi, qi, hi)),
            pl.BlockSpec((1, s, w), lambda bi, hi, qi: (bi, 0, hi)),
            pl.BlockSpec((1, s, w), lambda bi, hi, qi: (bi, 0, hi)),
            pl.BlockSpec((1, n_hh, n_q, tq), lambda bi, hi, qi: (bi, hi, 0, 0)),
        ],
        out_specs=pl.BlockSpec((1, tq, w), lambda bi, hi, qi: (bi, qi, hi)),
        out_shape=jax.ShapeDtypeStruct((b, s, d), BF16),
        scratch_shapes=[
            pltpu.VMEM((n_hh, tq, LANES), F32),
            pltpu.VMEM((n_hh, tq, LANES), F32),
            pltpu.VMEM((n_hh, tq, HEAD_DIM), F32),
        ],
        compiler_params=_params("parallel", "parallel", "arbitrary"),
        name="fox_prompt_attention",
    )(q, k, v, c4)


def _lane_scan(x, lane, step, limit=LANES, reverse=False, cyclic=False):
    shift = step
    while shift < limit:
        if cyclic:
            x = x + pltpu.roll(x, shift, axis=1)
        elif reverse:
            x = x + jnp.where(lane + shift < limit, pltpu.roll(x, LANES - shift, axis=1), 0.0)
        else:
            x = x + jnp.where(lane >= shift, pltpu.roll(x, shift, axis=1), 0.0)
        shift *= 2
    return x


def _sublane_suffix(x, row, cyclic=False):
    shift = 1
    while shift < SUBLANES:
        rolled = pltpu.roll(x, SUBLANES - shift, axis=0)
        x = x + (rolled if cyclic else jnp.where(row + shift < SUBLANES, rolled, 0.0))
        shift *= 2
    return x


def _decode_body(pt_ref, q_ref, *refs, n_pp, n_groups, n_tok, n_hh):
    del pt_ref
    k_refs, v_refs, lf_refs = refs[:n_pp], refs[n_pp : 2 * n_pp], refs[2 * n_pp : 3 * n_pp]
    kn_ref, vn_ref, lfn_ref, o_ref, bias_ref, m_ref, l_ref, acc_ref, carry_ref = refs[3 * n_pp :]
    p = pl.program_id(1)
    n_halves = q_ref.shape[1]
    n_rows = n_tok * n_hh
    lane = lax.broadcasted_iota(jnp.int32, (SUBLANES, LANES), 1)
    sub = lax.broadcasted_iota(jnp.int32, (SUBLANES, LANES), 0)
    row_q = lax.broadcasted_iota(jnp.int32, (n_rows, LANES), 0)
    lane_q = lax.broadcasted_iota(jnp.int32, (n_rows, LANES), 1)
    same_head = row_q % n_hh == lane_q % n_hh
    head_mask = jnp.where(same_head, 0.0, NEG_INF)

    @pl.when(p == 0)
    def _():
        bias_ref[...] = jnp.full(bias_ref.shape, NEG_INF, F32)
        m_ref[...] = jnp.full(m_ref.shape, 0.5 * NEG_INF, F32)
        l_ref[...] = jnp.zeros(l_ref.shape, F32)
        acc_ref[...] = jnp.zeros(acc_ref.shape, F32)
        carry_ref[...] = jnp.zeros(carry_ref.shape, F32)

    def attend(half, kbs, vbs, biases):
        q = q_ref[0, half]
        ss = []
        for kb, bias in zip(kbs, biases):
            s = _nt_dot(q, kb)
            ss.append(jnp.concatenate(
                [s[:, a * LANES : (a + 1) * LANES] + bias_a for a, bias_a in enumerate(bias)], axis=1))
        m_prev = m_ref[half]
        m_new = m_prev
        for s in ss:
            m_new = jnp.maximum(m_new, jnp.max(s, axis=-1, keepdims=True))
        alpha = jnp.exp2(m_prev - m_new)
        l_new = alpha * l_ref[half]
        acc = alpha * acc_ref[half]
        for s, vb in zip(ss, vbs):
            pr = jnp.exp2(s - jnp.tile(m_new, (1, s.shape[1] // LANES)))
            l_new = l_new + jnp.sum(pr, axis=-1, keepdims=True)
            acc = acc + jnp.dot(pr.astype(BF16), vb, preferred_element_type=F32)
        l_ref[half] = l_new
        acc_ref[half] = acc
        m_ref[half] = m_new

    def half_rows(ref, half):
        x = ref[:, half]
        return x.reshape(x.shape[0] * x.shape[1], x.shape[2]).astype(BF16)

    for half in range(n_halves):
        attend(half,
               [half_rows(k_ref, half) for k_ref in k_refs],
               [half_rows(v_ref, half) for v_ref in v_refs],
               [[head_mask + bias_ref[i, half, a : a + 1, :] for a in range(SUBLANES)] for i in range(n_pp)])

    for half in range(n_halves):
        carry = carry_ref[half]
        for i, lf_ref in enumerate(lf_refs):
            lf = lf_ref[0, half] * LOG2_E
            later = jnp.where(lane + n_hh < LANES, pltpu.roll(lf, LANES - n_hh, axis=1), 0.0)
            in_row = _lane_scan(later, lane, n_hh, reverse=True)
            row_total = _lane_scan(lf, lane, n_hh, cyclic=True)
            below = jnp.where(sub + 1 < SUBLANES, pltpu.roll(row_total, SUBLANES - 1, axis=0), 0.0)
            bias_ref[i, half] = in_row + _sublane_suffix(below, sub) + carry
            carry = carry + _sublane_suffix(row_total, sub, cyclic=True)
        carry_ref[half] = carry

    @pl.when(p == n_groups)
    def _():
        for half in range(n_halves):
            c_new = _lane_scan(jnp.broadcast_to(lfn_ref[0, half], (SUBLANES, LANES)) * LOG2_E, lane, n_hh,
                               limit=n_rows)
            causal = same_head & (lane_q // n_hh <= row_q // n_hh)
            attend(half, [kn_ref[0, half]], [vn_ref[0, half]], [[jnp.where(causal, -c_new[0:1, :], NEG_INF)]])
            o_ref[0, half] = acc_ref[half] / l_ref[half]


def _fox_sample_attention(page_table, q, k_cache, v_cache, lf_tiles, k_new, v_new, lf_new, base, n_pp=8):
    b, n_halves, n_rows, hd = q.shape
    n_hh = k_cache.shape[2]
    n_pages = page_table.shape[1]
    page = k_cache.shape[0] // lf_tiles.shape[0]
    assert n_hh == SUBLANES and page * n_hh == SUBLANES * LANES and n_pages % n_pp == 0
    n_groups = n_pages // n_pp

    def page_of(bi, group, pt, slot):
        return base + pt[bi, n_pages - 1 - (group * n_pp + slot)]

    def page_kv(bi, pi, pt, *, slot):
        return (page_of(bi, jnp.maximum(pi - 1, 0), pt, slot), 0, 0, 0)

    def page_lf(bi, pi, pt, *, slot):
        return (page_of(bi, jnp.minimum(pi, n_groups - 1), pt, slot), 0, 0, 0)

    def per_seq(bi, pi, pt):
        return (bi, 0, 0, 0)

    slots = range(n_pp)
    kv_specs = [pl.BlockSpec((page, n_halves, n_hh, hd), functools.partial(page_kv, slot=i)) for i in slots]
    lf_specs = [pl.BlockSpec((1, n_halves, SUBLANES, LANES), functools.partial(page_lf, slot=i))
                for i in slots]
    state = pltpu.VMEM((n_halves, n_rows, LANES), F32)
    grid_spec = pltpu.PrefetchScalarGridSpec(
        num_scalar_prefetch=1,
        grid=(b, n_groups + 1),
        in_specs=[pl.BlockSpec((1, n_halves, n_rows, hd), per_seq), *kv_specs, *kv_specs, *lf_specs,
                  pl.BlockSpec((1, n_halves, LANES, hd), per_seq),
                  pl.BlockSpec((1, n_halves, LANES, hd), per_seq),
                  pl.BlockSpec((1, n_halves, 1, LANES), per_seq)],
        out_specs=pl.BlockSpec((1, n_halves, n_rows, hd), per_seq),
        scratch_shapes=[
            pltpu.VMEM((n_pp, n_halves, SUBLANES, LANES), F32),
            state, state, state,
            pltpu.VMEM((n_halves, SUBLANES, LANES), F32),
        ],
    )
    return pl.pallas_call(
        functools.partial(_decode_body, n_pp=n_pp, n_groups=n_groups, n_tok=n_rows // n_hh, n_hh=n_hh),
        grid_spec=grid_spec,
        out_shape=jax.ShapeDtypeStruct((b, n_halves, n_rows, hd), F32),
        compiler_params=_params("parallel", "arbitrary"),
        name="fox_sample_attention",
    )(page_table, q, *[k_cache] * n_pp, *[v_cache] * n_pp, *[lf_tiles] * n_pp, k_new, v_new, lf_new)


def _gate_windows(d_rnn, block_w):
    spans = []
    for c0 in range(0, d_rnn, GATE_TILE):
        n_lo, n_hi = c0 // block_w, (c0 + GATE_TILE - 1) // block_w
        spans.append((block_w * n_lo // LANES * LANES, block_w * (n_hi + 1)))
    win = max(-(-(hi - lo) // LANES) * LANES for lo, hi in spans)
    starts = [min(lo, d_rnn - win) for lo, _ in spans]
    assert all(s + win >= hi for s, (_, hi) in zip(starts, spans))
    return starts, win


def _pack_block_diag(w, starts, win):
    n_blocks, bw, _ = w.shape
    tiles = []
    for c, k0 in enumerate(starts):
        c0, c1 = c * GATE_TILE, (c + 1) * GATE_TILE
        tile = jnp.zeros((win, GATE_TILE), w.dtype)
        for n in range(c0 // bw, (c1 - 1) // bw + 1):
            g0, g1 = max(bw * n, c0), min(bw * (n + 1), c1)
            piece = w[n][:, g0 - bw * n : g1 - bw * n]
            r0 = bw * n - k0
            tile = tile + jnp.pad(piece, ((r0, win - r0 - bw), (g0 - c0, c1 - g1)))
        tiles.append(tile)
    return jnp.stack(tiles)


def _lru_body(rec_ref, gate_ref, cinit_ref, hinit_ref, cw_ref, cb_ref, wa_ref, wi_ref, ba_ref, bi_ref,
              lam_ref, y_ref, hlast_ref, prev_ref, h_ref, *, tm, n_valid, starts, win):
    @pl.when(pl.program_id(1) == 0)
    def _():
        prev_ref[...] = cinit_ref[0]
        h_ref[...] = hinit_ref[0]

    x = rec_ref[0]
    d_rnn = x.shape[1]
    prev = prev_ref[...]
    row8 = lax.broadcasted_iota(jnp.int32, (SUBLANES, 1), 0)
    row = lax.broadcasted_iota(jnp.int32, (tm, 1), 0)
    row_in_group = lax.broadcasted_iota(jnp.int32, (1, SUBLANES, 1), 1)

    def delayed(k):
        r = pltpu.roll(x, k, axis=0)
        head = jnp.where(row8 < k, pltpu.roll(prev, k, axis=0), r[:SUBLANES])
        return head if tm == SUBLANES else jnp.concatenate([head, r[SUBLANES:]], axis=0)

    conv = cb_ref[...] + delayed(3) * cw_ref[0:1, :]
    conv = conv + delayed(2) * cw_ref[1:2, :]
    conv = conv + delayed(1) * cw_ref[2:3, :]
    conv = conv + x * cw_ref[3:4, :]
    prev_ref[...] = x[tm - SUBLANES :, :]
    conv_bf = conv.astype(BF16)
    sp = _softplus(-lam_ref[...])
    h_prev = h_ref[...]

    h_last = []
    for c, k0 in enumerate(starts):
        sl = slice(c * GATE_TILE, (c + 1) * GATE_TILE)
        window = conv_bf[:, k0 : k0 + win]
        r = jax.nn.sigmoid(jnp.dot(window, wa_ref[c], preferred_element_type=F32) + ba_ref[:, sl])
        ig = jax.nn.sigmoid(jnp.dot(window, wi_ref[c], preferred_element_type=F32) + bi_ref[:, sl])
        log_a = (-C_RG * r) * sp[:, sl]
        a = jnp.exp(log_a)
        xin = (jnp.sqrt(-(jnp.tanh(log_a) * (a * a + 1.0))) * ig) * conv[:, sl]
        if n_valid < tm:
            a = jnp.where(row < n_valid, a, 1.0)
            xin = jnp.where(row < n_valid, xin, 0.0)
        a = a.reshape(tm // SUBLANES, SUBLANES, GATE_TILE)
        xin = xin.reshape(tm // SUBLANES, SUBLANES, GATE_TILE)
        shift = 1
        while shift < SUBLANES:
            a_sh = jnp.where(row_in_group < shift, 1.0, pltpu.roll(a, shift, axis=1))
            x_sh = jnp.where(row_in_group < shift, 0.0, pltpu.roll(xin, shift, axis=1))
            xin = a * x_sh + xin
            a = a * a_sh
            shift *= 2
        h_in = h_prev[0:1, sl]
        groups = []
        for g in range(tm // SUBLANES):
            h_g = a[g] * h_in + xin[g]
            groups.append(h_g)
            h_in = h_g[SUBLANES - 1 :, :]
        h = groups[0] if len(groups) == 1 else jnp.concatenate(groups, axis=0)
        y_ref[0, :, sl] = (h * _gelu_tanh(gate_ref[0, :, sl])).astype(y_ref.dtype)
        h_last.append(h_in)
    h_new = jnp.broadcast_to(jnp.concatenate(h_last, axis=1), (SUBLANES, d_rnn))
    h_ref[...] = h_new
    hlast_ref[0] = h_new


def _rglru(u, conv_init, h_init, conv_w, conv_b, wa_pack, wi_pack, b_a, b_i, lam,
           starts, win, tm, n_valid):
    b, t, r = u.shape[0], u.shape[1], u.shape[2] // 2
    n_tiles = wa_pack.shape[0]

    def const2(bi, ti):
        return (0, 0)

    def const3(bi, ti):
        return (0, 0, 0)

    def per_seq(bi, ti):
        return (bi, 0, 0)

    def tile(bi, ti):
        return (bi, ti, 0)

    return pl.pallas_call(
        functools.partial(_lru_body, tm=tm, n_valid=n_valid, starts=tuple(starts), win=win),
        grid=(b, t // tm),
        in_specs=[
            pl.BlockSpec((1, tm, r), lambda bi, ti: (bi, ti, 1)),
            pl.BlockSpec((1, tm, r), tile),
            pl.BlockSpec((1, SUBLANES, r), per_seq),
            pl.BlockSpec((1, SUBLANES, r), per_seq),
            pl.BlockSpec((SUBLANES, r), const2),
            pl.BlockSpec((1, r), const2),
            pl.BlockSpec((n_tiles, win, GATE_TILE), const3),
            pl.BlockSpec((n_tiles, win, GATE_TILE), const3),
            pl.BlockSpec((1, r), const2),
            pl.BlockSpec((1, r), const2),
            pl.BlockSpec((1, r), const2),
        ],
        out_specs=[
            pl.BlockSpec((1, tm, r), tile),
            pl.BlockSpec((1, SUBLANES, r), per_seq),
        ],
        out_shape=[
            jax.ShapeDtypeStruct((b, t, r), BF16),
            jax.ShapeDtypeStruct((b, SUBLANES, r), F32),
        ],
        scratch_shapes=[pltpu.VMEM((SUBLANES, r), F32), pltpu.VMEM((SUBLANES, r), F32)],
        compiler_params=_params("parallel", "arbitrary"),
        name="rglru",
    )(u, u, conv_init, h_init, conv_w, conv_b, wa_pack, wi_pack, b_a, b_i, lam)


def _fox_project(xn, w_qkv, w_f, b_f, layer, d, k_stack, v_stack):
    n_layers = w_qkv.shape[0]
    q, = _matmul(xn, w_qkv, layer, 0, d, 1024, _epi_store_query, [BF16])
    k_stack, k16 = _kv_projection(xn, w_qkv, layer, d, k_stack, n_layers)
    v_stack, v16 = _kv_projection(xn, w_qkv, layer, 2 * d, v_stack, n_layers)
    lf, = _matmul(xn, w_f, layer, 0, LANES, LANES, _epi_log_sigmoid, [F32], bias=b_f)
    return q, k16, v16, lf, k_stack, v_stack


def kernel(x_prompt, x_sample, cache_k, cache_v, cache_logf, state_conv, state_h, page_table,
           norm_mix_pre, norm_mix_post, norm_ffn_pre, norm_ffn_post,
           fox_w_qkv, fox_w_f, fox_b_f, fox_w_o,
           lru_w_in, lru_conv_w, lru_conv_b, lru_w_a, lru_b_a, lru_w_i, lru_b_i, lru_lam, lru_w_out,
           ffn_w_gate, ffn_w_up, ffn_w_down):
    batch, seq, d = x_prompt.shape
    dec_batch, dec_seq, _ = x_sample.shape
    depth = norm_mix_pre.shape[0]
    n_heads = d // HEAD_DIM
    n_fox, n_pool, page = cache_logf.shape[:3]
    d_rnn = lru_lam.shape[1]
    block_w = lru_w_a.shape[2]
    m_p = batch * seq
    m_s = dec_batch * SAMPLE_ROWS
    pad_rows = SAMPLE_ROWS - dec_seq

    xp = x_prompt.reshape(m_p, d)
    xs = jnp.pad(x_sample, ((0, 0), (0, pad_rows), (0, 0))).reshape(m_s, d)

    halves = n_heads // SUBLANES
    k_cache = cache_k.reshape(n_fox * n_pool * page, halves, SUBLANES, HEAD_DIM)
    v_cache = cache_v.reshape(n_fox * n_pool * page, halves, SUBLANES, HEAD_DIM)
    lf_tiles = cache_logf.reshape(n_fox * n_pool, page, halves, SUBLANES).transpose(0, 2, 1, 3)
    lf_tiles = lf_tiles.reshape(n_fox * n_pool, halves, SUBLANES, LANES)
    starts, win = _gate_windows(d_rnn, block_w)

    w_qkv = fox_w_qkv.astype(BF16)
    w_f = jnp.pad(fox_w_f, ((0, 0), (0, 0), (0, LANES - n_heads))).astype(BF16)
    w_o = fox_w_o.astype(BF16)
    w_in = lru_w_in.astype(BF16)
    w_out = lru_w_out.astype(BF16)

    lp_l, ls_l = [], []
    cp_l, hp_l, cs_l, hs_l = [], [], [], []
    kp_stack = vp_stack = ks_stack = vs_stack = None
    xn_p = _rms_norm(xp, norm_mix_pre[0])
    xn_s = _rms_norm(xs, norm_mix_pre[0])
    for i in range(depth):
        j = i // 2
        g_post, g_ffn = norm_mix_post[i], norm_ffn_pre[i]
        g_next = norm_mix_pre[min(i + 1, depth - 1)]
        if i % 2 == 0:
            b_f = jnp.pad(fox_b_f[j], (0, LANES - n_heads)).reshape(1, LANES)

            q, k16, v16, lf, kp_stack, vp_stack = _fox_project(xn_p, w_qkv, w_f, b_f, j, d, kp_stack, vp_stack)
            lf = lf[:, :n_heads].reshape(batch, seq, n_heads)
            c = _cumsum_lanes(lf.transpose(0, 2, 1).reshape(batch * n_heads, seq))
            o = _fox_prompt_attention(q.reshape(batch, seq, d), k16.reshape(batch, seq, d),
                                      v16.reshape(batch, seq, d), c.reshape(batch, n_heads, seq))
            xp, xn_p = _matmul_post_norm(o.reshape(m_p, d), w_o, j, g_post, g_ffn, xp)
            lp_l.append(lf)

            q, k16, v16, lf, ks_stack, vs_stack = _fox_project(xn_s, w_qkv, w_f, b_f, j, d, ks_stack, vs_stack)
            lf = lf[:, :n_heads].reshape(dec_batch, SAMPLE_ROWS, n_heads)[:, :dec_seq]
            def split_heads(x, rows):
                x = x.reshape(dec_batch, SAMPLE_ROWS, halves, SUBLANES, HEAD_DIM)[:, :rows]
                return x.transpose(0, 2, 1, 3, 4).reshape(dec_batch, halves, rows * SUBLANES, HEAD_DIM)

            pad_new = ((0, 0), (0, 0), (0, LANES - SAMPLE_ROWS * SUBLANES), (0, 0))
            lf_new = jnp.pad(lf, ((0, 0), (0, pad_rows), (0, 0)))
            lf_new = lf_new.reshape(dec_batch, SAMPLE_ROWS, halves, SUBLANES).transpose(0, 2, 1, 3)
            lf_new = jnp.pad(lf_new.reshape(dec_batch, halves, 1, SAMPLE_ROWS * SUBLANES),
                             ((0, 0), (0, 0), (0, 0), (0, LANES - SAMPLE_ROWS * SUBLANES)))
            o = _fox_sample_attention(
                page_table, split_heads(q, dec_seq), k_cache, v_cache, lf_tiles,
                jnp.pad(split_heads(k16, SAMPLE_ROWS), pad_new),
                jnp.pad(split_heads(v16, SAMPLE_ROWS), pad_new),
                lf_new, j * n_pool)
            o = o.reshape(dec_batch, halves, dec_seq, SUBLANES, HEAD_DIM).transpose(0, 2, 1, 3, 4)
            o = jnp.pad(o.reshape(dec_batch, dec_seq, d), ((0, 0), (0, pad_rows), (0, 0)))
            xs, xn_s = _matmul_post_norm(o.reshape(m_s, d).astype(BF16), w_o, j, g_post, g_ffn, xs)
            ls_l.append(lf)
        else:
            conv_w = jnp.pad(lru_conv_w[j], ((0, SUBLANES - CONV_W), (0, 0)))
            conv_b = lru_conv_b[j].reshape(1, d_rnn)
            wa_pack = _pack_block_diag(lru_w_a[j], starts, win).astype(BF16)
            wi_pack = _pack_block_diag(lru_w_i[j], starts, win).astype(BF16)
            b_a = lru_b_a[j].reshape(1, d_rnn)
            b_i = lru_b_i[j].reshape(1, d_rnn)
            lam = lru_lam[j].reshape(1, d_rnn)

            def mixer(xn, n_seq, t, conv_init, h_init, tm, n_valid):
                u, = _matmul(xn, w_in, j, 0, 2 * d_rnn, d_rnn // 2, _epi_store, [F32])
                u = u.reshape(n_seq, t, 2 * d_rnn)
                y, h_last = _rglru(u, conv_init, h_init, conv_w, conv_b,
                                   wa_pack, wi_pack, b_a, b_i, lam, starts, win, tm, n_valid)
                return y.reshape(n_seq * t, d_rnn), u[:, :, d_rnn:], h_last[:, 0]

            zeros = jnp.zeros((batch, SUBLANES, d_rnn), F32)
            y, rec, h_last = mixer(xn_p, batch, seq, zeros, zeros, 256, 256)
            xp, xn_p = _matmul_post_norm(y, w_out, j, g_post, g_ffn, xp)
            cp_l.append(rec[:, seq - (CONV_W - 1) :])
            hp_l.append(h_last)

            conv_init = jnp.pad(state_conv[j], ((0, 0), (SUBLANES - (CONV_W - 1), 0), (0, 0)))
            h_init = jnp.pad(state_h[j][:, None, :], ((0, 0), (0, SUBLANES - 1), (0, 0)))
            y, rec, h_last = mixer(xn_s, dec_batch, SAMPLE_ROWS, conv_init, h_init, SAMPLE_ROWS, dec_seq)
            xs, xn_s = _matmul_post_norm(y, w_out, j, g_post, g_ffn, xs)
            full = jnp.concatenate([state_conv[j], rec[:, :dec_seq]], axis=1)
            cs_l.append(full[:, dec_seq:])
            hs_l.append(h_last)

        xs, xn_s, w_gate_up, w_down = _ffn_and_cast(xn_s, xs, ffn_w_gate, ffn_w_up, ffn_w_down, i,
                                                    norm_ffn_post[i], g_next)
        xp, xn_p = _ffn(xn_p, xp, w_gate_up, w_down, norm_ffn_post[i], g_next)

    y_sample = xs.reshape(dec_batch, SAMPLE_ROWS, d)[:, :dec_seq]
    prompt_kv = (n_fox, batch, seq, n_heads, HEAD_DIM)
    sample_kv = (n_fox, dec_batch, SAMPLE_ROWS, n_heads, HEAD_DIM)
    return (xp.reshape(batch, seq, d), y_sample,
            kp_stack.reshape(prompt_kv), vp_stack.reshape(prompt_kv), jnp.stack(lp_l),
            ks_stack.reshape(sample_kv)[:, :, :dec_seq], vs_stack.reshape(sample_kv)[:, :, :dec_seq],
            jnp.stack(ls_l),
            jnp.stack(cp_l), jnp.stack(hp_l), jnp.stack(cs_l), jnp.stack(hs_l))
```
